```python
import math
import jax, jax.numpy as jnp
from jax import lax
import numpy as np

D_MODEL = 1024
BATCH = 16
SEQ = 4096
DEPTH = 1
DEC_BATCH = 8
DEC_SEQ = 16
PAST_LEN = 4096

CHUNK = 64
N_HEADS = 8
N_KV_HEADS = 2
HEAD_DIM = 64
GROUP = N_HEADS // N_KV_HEADS
ATTN_DIM = N_HEADS * HEAD_DIM
KV_DIM = N_KV_HEADS * HEAD_DIM
IDX_HEADS = 4
IDX_DIM = 64
TOPK_MAX = 256
Q_BLOCK = 128
CONV_DIM = 512
CONV_W = 3
N_BUCKETS = 32
MAX_DISTANCE = 128
N_GROUPS = 4
EXPERTS_PER_GROUP = 4
N_EXPERTS = N_GROUPS * EXPERTS_PER_GROUP
TOP_K_EXPERTS = 2
D_EXPERT = 512
EPS = 1e-6
ATTN_SCALE = HEAD_DIM ** -0.5
IDX_SCALE = IDX_DIM ** -0.5
IDX_W_SCALE = IDX_HEADS ** -0.5
SPLITS = (ATTN_DIM, KV_DIM, KV_DIM, IDX_HEADS * IDX_DIM, IDX_DIM, IDX_HEADS,
          CONV_DIM, CONV_DIM, CONV_DIM, D_MODEL, D_MODEL)
D_IN = sum(SPLITS)

kernel_name = "dsa_shortconv_hmoe_streaming_step"


def rmsnorm(x, g):
    xf = x.astype(jnp.float32)
    y = xf * lax.rsqrt(jnp.mean(xf * xf, axis=-1, keepdims=True) + EPS)
    return (y * g.astype(jnp.float32)).astype(x.dtype)


def t5_bucket(rel):
    nb = N_BUCKETS // 2
    max_exact = nb // 2
    ret = jnp.where(rel > 0, nb, 0)
    n = jnp.abs(rel)
    nf = jnp.maximum(n, 1).astype(jnp.float32)
    large = max_exact + (jnp.log(nf / max_exact) / math.log(MAX_DISTANCE / max_exact)
                         * (nb - max_exact)).astype(jnp.int32)
    large = jnp.minimum(large, nb - 1)
    return ret + jnp.where(n < max_exact, n, large)


def dsa_block(q, qi, wi, q_pos, k, v, ki, top_k, rel_bias):
    B, Q = q.shape[:2]
    L = k.shape[1]
    q_chunk = q_pos // CHUNK
    k_chunk = jnp.arange(L, dtype=jnp.int32) // CHUNK
    adm = k_chunk[None, :] <= q_chunk[:, None]
    s = jnp.einsum('bqhd,bld->bqhl', qi, ki).astype(jnp.float32) * IDX_SCALE
    score = jnp.einsum('bqhl,bqh->bql', jax.nn.relu(s), wi.astype(jnp.float32))
    score = jnp.where(adm[None], score, -jnp.inf)
    _, idx = lax.top_k(score, top_k)
    gather = jax.vmap(lambda rows, i: rows[i])
    ks = gather(k, idx)
    vs = gather(v, idx)
    valid = (idx // CHUNK) <= q_chunk[None, :, None]
    bias = rel_bias[t5_bucket(idx - q_pos[None, :, None])]
    bias = bias.reshape(B, Q, top_k, N_KV_HEADS, GROUP).transpose(0, 1, 3, 4, 2)
    qg = q.reshape(B, Q, N_KV_HEADS, GROUP, HEAD_DIM)
    logits = (jnp.einsum('bqngd,bqknd->bqngk', qg, ks).astype(jnp.float32) * ATTN_SCALE
              + bias.astype(jnp.float32))
    logits = jnp.where(valid[:, :, None, None, :], logits, -jnp.inf)
    p = jax.nn.softmax(logits, axis=-1).astype(v.dtype)
    o = jnp.einsum('bqngk,bqknd->bqngd', p, vs)
    return o.reshape(B, Q, ATTN_DIM)


def dsa_prompt(q, qi, wi, k, v, ki, rel_bias):
    B, S = q.shape[:2]
    nb = S // Q_BLOCK
    top_k = min(TOPK_MAX, S // 4)

    def blocks(a):
        return a.reshape((B, nb, Q_BLOCK) + a.shape[2:]).swapaxes(0, 1)

    q_pos = jnp.arange(S, dtype=jnp.int32).reshape(nb, Q_BLOCK)

    def body(args):
        qb, qib, wib, pb = args
        return dsa_block(qb, qib, wib, pb, k, v, ki, top_k, rel_bias)

    o = lax.map(body, (blocks(q), blocks(qi), blocks(wi), q_pos))
    return o.swapaxes(0, 1).reshape(B, S, ATTN_DIM)


def token_mixers(h, past, w_in, w_attn_out, w_conv, w_conv_out, w_out, rel_bias):
    B, T, _ = h.shape
    points = [int(p) for p in np.cumsum(SPLITS)[:-1]]
    q, k, v, qi, ki, wi, cb, cc, cx, ga, gc = jnp.split(h @ w_in, points, axis=-1)
    q = q.reshape(B, T, N_HEADS, HEAD_DIM)
    k = k.reshape(B, T, N_KV_HEADS, HEAD_DIM)
    v = v.reshape(B, T, N_KV_HEADS, HEAD_DIM)
    qi = qi.reshape(B, T, IDX_HEADS, IDX_DIM)
    wi = wi * IDX_W_SCALE
    u = cc * cx
    if past is None:
        attn = dsa_prompt(q, qi, wi, k, v, ki, rel_bias)
        conv_buf = jnp.zeros((B, CONV_W - 1, CONV_DIM), u.dtype)
    else:
        past_k, past_v, past_ki, conv_buf = past
        P = past_k.shape[1]
        k_all = jnp.concatenate([past_k.astype(k.dtype), k], axis=1)
        v_all = jnp.concatenate([past_v.astype(v.dtype), v], axis=1)
        ki_all = jnp.concatenate([past_ki.astype(ki.dtype), ki], axis=1)
        q_pos = P + jnp.arange(T, dtype=jnp.int32)
        attn = dsa_block(q, qi, wi, q_pos, k_all, v_all, ki_all, min(TOPK_MAX, (P + T) // 4), rel_bias)
    full = jnp.concatenate([conv_buf.astype(u.dtype), u], axis=1)
    conv = w_conv[0] * full[:, 0:T]
    for j in range(1, CONV_W):
        conv = conv + w_conv[j] * full[:, j:j + T]
    y_conv = (cb * conv) @ w_conv_out
    y_attn = attn @ w_attn_out
    merged = jax.nn.sigmoid(ga) * y_attn + jax.nn.sigmoid(gc) * y_conv
    return merged @ w_out, (k, v, ki, full[:, T:])


def hier_moe(h, w_group, b_group, w_router, b_router, w_gate, w_up, w_down):
    shp = h.shape
    hf = h.reshape(-1, D_MODEL)
    n = hf.shape[0]
    g_logits = (hf @ w_group).astype(jnp.float32)
    g_sel = jnp.argmax(g_logits + b_group.astype(jnp.float32), axis=-1)
    g_prob = jnp.take_along_axis(jax.nn.softmax(g_logits, axis=-1), g_sel[:, None], axis=-1)
    e_logits = (hf @ w_router).astype(jnp.float32).reshape(n, N_GROUPS, EXPERTS_PER_GROUP)
    e_in = jnp.take_along_axis(e_logits, g_sel[:, None, None], axis=1)[:, 0]
    b_in = b_router.astype(jnp.float32).reshape(N_GROUPS, EXPERTS_PER_GROUP)[g_sel]
    _, e_idx = lax.top_k(e_in + b_in, TOP_K_EXPERTS)
    e_prob = jnp.take_along_axis(jax.nn.softmax(e_in, axis=-1), e_idx, axis=-1)
    wts = g_prob * e_prob / jnp.sum(e_prob, axis=-1, keepdims=True)
    eid = g_sel[:, None] * EXPERTS_PER_GROUP + e_idx
    gates = jnp.einsum('nk,nke->ne', wts, jax.nn.one_hot(eid, N_EXPERTS, dtype=jnp.float32)).astype(h.dtype)
    out = jnp.zeros_like(hf)
    for e in range(N_EXPERTS):
        act = jax.nn.silu(hf @ w_gate[e]) * (hf @ w_up[e])
        out = out + gates[:, e:e + 1] * (act @ w_down[e])
    return out.reshape(shp)


def trunk_layer(x, past, lw, rel_bias):
    (norm_mix, w_in, w_attn_out, w_conv, w_conv_out, w_out, norm_ffn,
     w_group, b_group, w_router, b_router, w_gate, w_up, w_down) = lw
    mix, new_state = token_mixers(rmsnorm(x, norm_mix), past, w_in, w_attn_out, w_conv,
                                  w_conv_out, w_out, rel_bias)
    x = x + mix
    x = x + hier_moe(rmsnorm(x, norm_ffn), w_group, b_group, w_router, b_router, w_gate, w_up, w_down)
    return x, new_state


def setup_inputs(seed: int = 0) -> dict:
    key = jax.random.key(seed)
    ks = jax.random.split(key, 24)

    def nrm(k, shape, scale):
        return jax.random.normal(k, shape, jnp.float32) * scale

    return {
        "x_prompt": nrm(ks[0], (BATCH, SEQ, D_MODEL), 1.0),
        "x_sample": nrm(ks[1], (DEC_BATCH, DEC_SEQ, D_MODEL), 1.0),
        "cache_k": nrm(ks[2], (DEPTH, DEC_BATCH, PAST_LEN, N_KV_HEADS, HEAD_DIM), 1.0),
        "cache_v": nrm(ks[3], (DEPTH, DEC_BATCH, PAST_LEN, N_KV_HEADS, HEAD_DIM), 1.0),
        "cache_kidx": nrm(ks[4], (DEPTH, DEC_BATCH, PAST_LEN, IDX_DIM), 1.0),
        "state_conv": nrm(ks[5], (DEPTH, DEC_BATCH, CONV_W - 1, CONV_DIM), 1.0),
        "rel_bias": nrm(ks[6], (N_BUCKETS, N_HEADS), 0.1),
        "norm_mix": 1.0 + nrm(ks[7], (DEPTH, D_MODEL), 0.02),
        "w_in": nrm(ks[8], (DEPTH, D_MODEL, D_IN), D_MODEL ** -0.5),
        "w_attn_out": nrm(ks[9], (DEPTH, ATTN_DIM, D_MODEL), ATTN_DIM ** -0.5),
        "w_conv": nrm(ks[10], (DEPTH, CONV_W, CONV_DIM), CONV_W ** -0.5),
        "w_conv_out": nrm(ks[11], (DEPTH, CONV_DIM, D_MODEL), CONV_DIM ** -0.5),
        "w_out": nrm(ks[12], (DEPTH, D_MODEL, D_MODEL), D_MODEL ** -0.5),
        "norm_ffn": 1.0 + nrm(ks[13], (DEPTH, D_MODEL), 0.02),
        "w_group": nrm(ks[14], (DEPTH, D_MODEL, N_GROUPS), D_MODEL ** -0.5),
        "b_group": nrm(ks[15], (DEPTH, N_GROUPS), 0.01),
        "w_router": nrm(ks[16], (DEPTH, D_MODEL, N_EXPERTS), D_MODEL ** -0.5),
        "b_router": nrm(ks[17], (DEPTH, N_EXPERTS), 0.01),
        "w_gate": nrm(ks[18], (DEPTH, N_EXPERTS, D_MODEL, D_EXPERT), D_MODEL ** -0.5),
        "w_up": nrm(ks[19], (DEPTH, N_EXPERTS, D_MODEL, D_EXPERT), D_MODEL ** -0.5),
        "w_down": nrm(ks[20], (DEPTH, N_EXPERTS, D_EXPERT, D_MODEL), D_EXPERT ** -0.5),
        "norm_final": 1.0 + nrm(ks[21], (D_MODEL,), 0.02),
    }


def reference(x_prompt, x_sample, cache_k, cache_v, cache_kidx, state_conv, rel_bias,
              norm_mix, w_in, w_attn_out, w_conv, w_conv_out, w_out, norm_ffn,
              w_group, b_group, w_router, b_router, w_gate, w_up, w_down, norm_final):
    xp, xs = x_prompt, x_sample
    kp, vp, kip, cp = [], [], [], []
    ksm, vsm, kism, csm = [], [], [], []
    for l in range(DEPTH):
        lw = (norm_mix[l], w_in[l], w_attn_out[l], w_conv[l], w_conv_out[l], w_out[l], norm_ffn[l],
              w_group[l], b_group[l], w_router[l], b_router[l], w_gate[l], w_up[l], w_down[l])
        xp, st_p = trunk_layer(xp, None, lw, rel_bias)
        xs, st_s = trunk_layer(xs, (cache_k[l], cache_v[l], cache_kidx[l], state_conv[l]), lw, rel_bias)
        kp.append(st_p[0]); vp.append(st_p[1]); kip.append(st_p[2]); cp.append(st_p[3])
        ksm.append(st_s[0]); vsm.append(st_s[1]); kism.append(st_s[2]); csm.append(st_s[3])
    y_prompt = rmsnorm(xp, norm_final)
    y_sample = rmsnorm(xs, norm_final)
    k_prompt, v_prompt, kidx_prompt, conv_prompt = jnp.stack(kp), jnp.stack(vp), jnp.stack(kip), jnp.stack(cp)
    k_sample, v_sample, kidx_sample, conv_sample = jnp.stack(ksm), jnp.stack(vsm), jnp.stack(kism), jnp.stack(csm)
    return (y_prompt, y_sample, k_prompt, v_prompt, kidx_prompt, conv_prompt,
            k_sample, v_sample, kidx_sample, conv_sample)
```

```python
import functools
import math

import jax
import jax.numpy as jnp
import numpy as np
from jax import lax
from jax.experimental import pallas as pl
from jax.experimental.pallas import tpu as pltpu

F32 = jnp.float32
BF16 = jnp.bfloat16
I32 = jnp.int32

CHUNK = 64
N_HEADS = 8
N_KV_HEADS = 2
GROUP = N_HEADS // N_KV_HEADS
HEAD_DIM = 64
ATTN_DIM = N_HEADS * HEAD_DIM
KV_DIM = N_KV_HEADS * HEAD_DIM
IDX_HEADS = 4
IDX_DIM = 64
TOPK_MAX = 256
CONV_DIM = 512
CONV_W = 3
N_BUCKETS = 32
MAX_DISTANCE = 128
N_GROUPS = 4
EXPERTS_PER_GROUP = 4
N_EXPERTS = N_GROUPS * EXPERTS_PER_GROUP
EPS = 1e-6
ATTN_SCALE = HEAD_DIM ** -0.5
IDX_SCALE = IDX_DIM ** -0.5
IDX_W_SCALE = IDX_HEADS ** -0.5

LANES = 128
KEY_TILE = 256
VMEM_LIMIT = 56 * 1024 * 1024
INT_MIN = -2 ** 31
IDX_BIG = 2 ** 30


def _dot(a, b):
    return jnp.dot(a, b, preferred_element_type=F32)


def _dot_nt(a, b):
    return lax.dot_general(a, b, (((1,), (1,)), ((), ())), preferred_element_type=F32)


def _sigmoid(x):
    return 1.0 / (1.0 + jnp.exp(-x))


def _rms(x, g):
    return x * lax.rsqrt(jnp.mean(x * x, axis=-1, keepdims=True) + EPS) * g


def _proj_kernel(x_ref, cinit_ref, g_ref, wa_ref, wb_ref, wc_ref, wg_ref, wconv_ref, wco_ref,
                 q_ref, k_ref, v_ref, kb_ref, vb_ref, qi_ref, kidx_ref, kid_ref, wi_ref,
                 sga_ref, pc_ref, cst_ref, carry_ref, *, tm):
    j = pl.program_id(1)
    hb = _rms(x_ref[0], g_ref[...]).astype(BF16)

    a = _dot(hb, wa_ref[...])
    q_ref[0] = a[:, :ATTN_DIM].astype(BF16)
    k = a[:, ATTN_DIM:ATTN_DIM + KV_DIM]
    v = a[:, ATTN_DIM + KV_DIM:]
    k_ref[0] = k
    v_ref[0] = v
    kb_ref[0] = k.astype(BF16)
    vb_ref[0] = v.astype(BF16)

    b = _dot(hb, wb_ref[...])
    nqi = IDX_HEADS * IDX_DIM
    qi_ref[0] = b[:, :nqi].astype(BF16)
    kid_ref[0] = b[:, nqi:nqi + LANES].astype(BF16)
    kidx_ref[0] = b[:, nqi:nqi + IDX_DIM]
    wi_ref[0] = b[:, nqi + LANES:]

    c = _dot(hb, wc_ref[...])
    cb = c[:, :CONV_DIM]
    u = c[:, CONV_DIM:2 * CONV_DIM] * c[:, 2 * CONV_DIM:]

    @pl.when(j == 0)
    def _():
        carry_ref[6:8, :] = cinit_ref[0]

    c2 = carry_ref[6:7, :]
    c1 = carry_ref[7:8, :]
    row = lax.broadcasted_iota(I32, (tm, CONV_DIM), 0)
    up1 = jnp.where(row == 0, c1, pltpu.roll(u, 1, axis=0))
    up2 = jnp.where(row == 0, c2, jnp.where(row == 1, c1, pltpu.roll(u, 2, axis=0)))
    conv = wconv_ref[0:1, :] * up2 + wconv_ref[1:2, :] * up1 + wconv_ref[2:3, :] * u
    tail = u[tm - 2:tm, :]
    carry_ref[6:8, :] = tail
    cst_ref[0] = tail
    yc = _dot((cb * conv).astype(BF16), wco_ref[...])

    gt = _dot(hb, wg_ref[...])
    d = yc.shape[-1]
    sga_ref[0] = _sigmoid(gt[:, :d])
    pc_ref[0] = _sigmoid(gt[:, d:]) * yc


def _proj_call(x, cinit, g, wa, wb, wc, wg, wconv, wco, tm):
    bsz, s, d = x.shape
    grid = (bsz, s // tm)
    tok = lambda n: pl.BlockSpec((1, tm, n), lambda b, j: (b, j, 0))
    full = lambda a: pl.BlockSpec(a.shape, lambda b, j: (0,) * a.ndim)
    st = pl.BlockSpec((1, CONV_W - 1, CONV_DIM), lambda b, j: (b, 0, 0))
    out_shape = (
        jax.ShapeDtypeStruct((bsz, s, ATTN_DIM), BF16),
        jax.ShapeDtypeStruct((bsz, s, KV_DIM), F32),
        jax.ShapeDtypeStruct((bsz, s, KV_DIM), F32),
        jax.ShapeDtypeStruct((bsz, s, KV_DIM), BF16),
        jax.ShapeDtypeStruct((bsz, s, KV_DIM), BF16),
        jax.ShapeDtypeStruct((bsz, s, IDX_HEADS * IDX_DIM), BF16),
        jax.ShapeDtypeStruct((bsz, s, IDX_DIM), F32),
        jax.ShapeDtypeStruct((bsz, s, LANES), BF16),
        jax.ShapeDtypeStruct((bsz, s, LANES), F32),
        jax.ShapeDtypeStruct((bsz, s, d), F32),
        jax.ShapeDtypeStruct((bsz, s, d), F32),
        jax.ShapeDtypeStruct((bsz, CONV_W - 1, CONV_DIM), F32),
    )
    out_specs = (tok(ATTN_DIM), tok(KV_DIM), tok(KV_DIM), tok(KV_DIM), tok(KV_DIM),
                 tok(IDX_HEADS * IDX_DIM), tok(IDX_DIM), tok(LANES), tok(LANES), tok(d), tok(d), st)
    return pl.pallas_call(
        functools.partial(_proj_kernel, tm=tm),
        grid=grid,
        in_specs=[tok(d), st, full(g), full(wa), full(wb), full(wc), full(wg), full(wconv), full(wco)],
        out_specs=out_specs,
        out_shape=out_shape,
        scratch_shapes=[pltpu.VMEM((8, CONV_DIM), F32)],
        compiler_params=pltpu.CompilerParams(
            dimension_semantics=("arbitrary", "arbitrary"), vmem_limit_bytes=VMEM_LIMIT),
        name="proj",
    )(x, cinit, g, wa, wb, wc, wg, wconv, wco)


def _dsa_kernel(q_ref, qi_ref, wi_ref, kid_ref, kb_ref, vb_ref, bt_ref, out_ref,
                keys_ref, tidx_ref, qis_ref, ws_ref, qs_ref, m_ref, l_ref, acc_ref,
                *, tq, tk, q_off, n_valid_last, topk, idx_bits):
    i = pl.program_id(1)
    nt = (q_off + i * tq) // tk + 1
    r4 = GROUP * tq
    neg_inf = F32(-jnp.inf)

    lane = lax.broadcasted_iota(I32, (tq, LANES), 1)
    lo = lane < HEAD_DIM
    zero_b = jnp.zeros((tq, LANES), BF16)
    qi = qi_ref[0]
    for h in range(IDX_HEADS):
        blk = qi[:, (h // 2) * LANES:(h // 2 + 1) * LANES]
        qis_ref[h * tq:(h + 1) * tq, :] = jnp.where(lo if h % 2 == 0 else ~lo, blk, zero_b)
        ws_ref[h * tq:(h + 1) * tq, :] = jnp.broadcast_to(wi_ref[0][:, h:h + 1], (tq, tk))
    q = q_ref[0]
    for n in range(N_KV_HEADS):
        for g in range(GROUP):
            blk = q[:, g * LANES:(g + 1) * LANES]
            qs_ref[n, g * tq:(g + 1) * tq, :] = jnp.where(lo if n == 0 else ~lo, blk, zero_b)

    row = lax.broadcasted_iota(I32, (tq, tk), 0)
    col = lax.broadcasted_iota(I32, (tq, tk), 1)
    lim_last = jnp.minimum(((row >> 6) + 1) << 6, n_valid_last)

    def score_body(t, carry):
        s = _dot_nt(qis_ref[...], kid_ref[0, t])
        s = jnp.maximum(s, 0.0) * ws_ref[...]
        sc = (s[0:tq] + s[tq:2 * tq]) + (s[2 * tq:3 * tq] + s[3 * tq:4 * tq])
        sc = jnp.where(sc == 0.0, 0.0, sc)
        bits = lax.bitcast_convert_type(sc, I32)
        key = bits ^ ((bits >> 31) & 0x7FFFFFFF)
        lim = jnp.where(t == nt - 1, lim_last, tk)
        keys_ref[t] = jnp.where(col < lim, key, INT_MIN)
        return carry

    lax.fori_loop(0, nt, score_body, 0)

    def count(ref, cand, cmp):
        cand_b = jnp.broadcast_to(cand, (tq, LANES))

        def body(t, acc):
            x = ref[t]
            for c in range(tk // LANES):
                acc = acc + jnp.where(cmp(x[:, c * LANES:(c + 1) * LANES], cand_b), 1, 0)
            return acc

        acc = lax.fori_loop(0, nt, body, jnp.zeros((tq, LANES), I32))
        return jnp.sum(acc.astype(F32), axis=-1, keepdims=True)

    ge = lambda x, c: x >= c
    lt = lambda x, c: x < c

    def bit_body(b, thr):
        cand = thr + (jnp.int32(1) << (31 - b))
        return jnp.where(count(keys_ref, cand, ge) >= topk, cand, thr)

    thr = lax.fori_loop(0, 32, bit_body, jnp.full((tq, 1), INT_MIN, I32))

    need = topk - count(keys_ref, thr + 1, ge)
    thr_b = jnp.broadcast_to(thr, (tq, tk))

    def tidx_body(t, carry):
        tidx_ref[t] = jnp.where(keys_ref[t] == thr_b, col + t * tk, IDX_BIG)
        return carry

    lax.fori_loop(0, nt, tidx_body, 0)

    def jbit_body(b, jj):
        cand = jj + (jnp.int32(1) << (idx_bits - 1 - b))
        return jnp.where(count(tidx_ref, cand, lt) <= need, cand, jj)

    jcut = lax.fori_loop(0, idx_bits, jbit_body, jnp.zeros((tq, 1), I32))
    jcut = jnp.where(thr == INT_MIN, 0, jcut)
    jcut_b = jnp.broadcast_to(jcut, (tq, tk))

    m_ref[...] = jnp.full(m_ref.shape, neg_inf, F32)
    l_ref[...] = jnp.zeros(l_ref.shape, F32)
    acc_ref[...] = jnp.zeros(acc_ref.shape, F32)

    def tile(t, near):
        sel = jnp.where(keys_ref[t] > thr_b, 0.0, jnp.where(tidx_ref[t] < jcut_b, 0.0, neg_inf))
        sel4 = jnp.concatenate([sel] * GROUP, axis=0)
        kt = kb_ref[0, t]
        vt = vb_ref[0, t]
        for n in range(N_KV_HEADS):
            lg = _dot_nt(qs_ref[n], kt) + sel4
            if near is not None:
                lg = lg + bt_ref[near, n]
            m_old = m_ref[n]
            mx = lg[:, 0:LANES]
            for c in range(1, tk // LANES):
                mx = jnp.maximum(mx, lg[:, c * LANES:(c + 1) * LANES])
            m_new = jnp.maximum(m_old, jnp.max(mx, axis=-1, keepdims=True))
            m_safe = jnp.where(m_new == neg_inf, 0.0, m_new)
            alpha = jnp.exp(m_old - m_safe)
            p = jnp.exp(lg - m_safe[:, 0:1])
            ps = p[:, 0:LANES]
            for c in range(1, tk // LANES):
                ps = ps + p[:, c * LANES:(c + 1) * LANES]
            m_ref[n] = m_new
            l_ref[n] = alpha * l_ref[n] + ps
            acc_ref[n] = alpha * acc_ref[n] + _dot(p.astype(BF16), vt)

    def far_body(t, carry):
        tile(t, None)
        return carry

    lax.fori_loop(0, nt - 2, far_body, 0)
    if isinstance(nt, int):
        if nt >= 2:
            tile(nt - 2, 0)
    else:
        @pl.when(nt >= 2)
        def _():
            tile(nt - 2, 0)
    tile(nt - 1, 1)

    o = []
    for n in range(N_KV_HEADS):
        o.append(acc_ref[n] / jnp.sum(l_ref[n], axis=-1, keepdims=True))
    for g in range(GROUP):
        blk = jnp.where(lo, o[0][g * tq:(g + 1) * tq, :], o[1][g * tq:(g + 1) * tq, :])
        out_ref[0, :, g * LANES:(g + 1) * LANES] = blk.astype(BF16)


def _dsa_call(q, qi, wi, kid, kb, vb, bt, *, tq, q_off, n_valid_last, topk):
    bsz, sq, _ = q.shape
    nt_max, tk = kid.shape[1], kid.shape[2]
    idx_bits = int(nt_max * tk).bit_length()
    r4 = GROUP * tq
    tok = lambda n: pl.BlockSpec((1, tq, n), lambda b, i: (b, i, 0))
    keys = pl.BlockSpec((1, nt_max, tk, LANES), lambda b, i: (b, 0, 0, 0))
    kern = functools.partial(_dsa_kernel, tq=tq, tk=tk, q_off=q_off, n_valid_last=n_valid_last,
                             topk=topk, idx_bits=idx_bits)
    return pl.pallas_call(
        kern,
        grid=(bsz, sq // tq),
        in_specs=[tok(ATTN_DIM), tok(IDX_HEADS * IDX_DIM), tok(LANES), keys, keys, keys,
                  pl.BlockSpec(bt.shape, lambda b, i: (0, 0, 0, 0))],
        out_specs=tok(ATTN_DIM),
        out_shape=jax.ShapeDtypeStruct((bsz, sq, ATTN_DIM), BF16),
        scratch_shapes=[
            pltpu.VMEM((nt_max, tq, tk), I32),
            pltpu.VMEM((nt_max, tq, tk), I32),
            pltpu.VMEM((IDX_HEADS * tq, LANES), BF16),
            pltpu.VMEM((IDX_HEADS * tq, tk), F32),
            pltpu.VMEM((N_KV_HEADS, r4, LANES), BF16),
            pltpu.VMEM((N_KV_HEADS, r4, LANES), F32),
            pltpu.VMEM((N_KV_HEADS, r4, LANES), F32),
            pltpu.VMEM((N_KV_HEADS, r4, LANES), F32),
        ],
        compiler_params=pltpu.CompilerParams(
            dimension_semantics=("arbitrary", "arbitrary"), vmem_limit_bytes=VMEM_LIMIT),
        name="dsa",
    )(q, qi, wi, kid, kb, vb, bt)


def _t5_bucket(rel):
    nb = N_BUCKETS // 2
    max_exact = nb // 2
    ret = jnp.where(rel > 0, nb, 0)
    n = jnp.abs(rel)
    nf = jnp.maximum(n, 1).astype(F32)
    large = max_exact + (jnp.log(nf / max_exact) / math.log(MAX_DISTANCE / max_exact)
                         * (nb - max_exact)).astype(I32)
    large = jnp.minimum(large, nb - 1)
    return ret + jnp.where(n < max_exact, n, large)


def _bias_tables(rel_bias, tq, tk):
    far_rel = jnp.full((1,), -(tk + 1), I32)
    far = rel_bias[_t5_bucket(far_rel)[0]]
    rel = (jnp.arange(2 * tk, dtype=I32)[None, :] - tk) - jnp.arange(tq, dtype=I32)[:, None]
    tab = rel_bias[_t5_bucket(rel)] - far
    tab = tab.reshape(tq, 2, tk, N_KV_HEADS, GROUP).transpose(1, 3, 4, 0, 2)
    return tab.reshape(2, N_KV_HEADS, GROUP * tq, tk).astype(F32)


def _mix_kernel(attn_ref, sga_ref, pc_ref, x_ref, wao_ref, wout_ref, g_ref, wr_ref, br_ref,
                x1_ref, h2_ref, gates_ref):
    ya = _dot(attn_ref[...], wao_ref[...])
    merged = sga_ref[...] * ya + pc_ref[...]
    x1 = x_ref[...] + _dot(merged.astype(BF16), wout_ref[...])
    x1_ref[...] = x1
    hb = _rms(x1, g_ref[...]).astype(BF16)
    h2_ref[...] = hb

    rl = _dot(hb, wr_ref[...])
    tm = rl.shape[0]
    lane = lax.broadcasted_iota(I32, (tm, LANES), 1)
    neg_inf = F32(-jnp.inf)
    big = I32(LANES)

    def first_argmax(x):
        mx = jnp.max(x, axis=-1, keepdims=True)
        return jnp.min(jnp.where(x == mx, lane, big), axis=-1, keepdims=True)

    def pick(x, idx):
        return jnp.sum(jnp.where(lane == idx, x, 0.0), axis=-1, keepdims=True)

    glog = rl[:, :LANES]
    gvalid = lane < N_GROUPS
    g_sel = first_argmax(jnp.where(gvalid, glog + br_ref[0:1, :], neg_inf))
    gm = jnp.max(jnp.where(gvalid, glog, neg_inf), axis=-1, keepdims=True)
    gexp = jnp.where(gvalid, jnp.exp(glog - gm), 0.0)
    g_prob = pick(gexp, g_sel) / jnp.sum(gexp, axis=-1, keepdims=True)

    elog = rl[:, LANES:]
    in_grp = (lane // EXPERTS_PER_GROUP) == g_sel
    eb = jnp.where(in_grp, elog + br_ref[1:2, :], neg_inf)
    i1 = first_argmax(eb)
    i2 = first_argmax(jnp.where(lane == i1, neg_inf, eb))
    em = jnp.max(jnp.where(in_grp, elog, neg_inf), axis=-1, keepdims=True)
    eexp = jnp.where(in_grp, jnp.exp(elog - em), 0.0)
    esum = jnp.sum(eexp, axis=-1, keepdims=True)
    p1 = pick(eexp, i1) / esum
    p2 = pick(eexp, i2) / esum
    w1 = g_prob * p1 / (p1 + p2)
    w2 = g_prob * p2 / (p1 + p2)
    gates_ref[...] = jnp.where(lane == i1, w1, 0.0) + jnp.where(lane == i2, w2, 0.0)


def _mix_call(attn, sga, pc, x, wao, wout, g, wr, br, tm):
    n, d = x.shape
    tok = lambda c: pl.BlockSpec((tm, c), lambda i: (i, 0))
    full = lambda a: pl.BlockSpec(a.shape, lambda i: (0,) * a.ndim)
    return pl.pallas_call(
        _mix_kernel,
        grid=(n // tm,),
        in_specs=[tok(ATTN_DIM), tok(d), tok(d), tok(d), full(wao), full(wout), full(g), full(wr), full(br)],
        out_specs=(tok(d), tok(d), tok(LANES)),
        out_shape=(jax.ShapeDtypeStruct((n, d), F32), jax.ShapeDtypeStruct((n, d), BF16),
                   jax.ShapeDtypeStruct((n, LANES), F32)),
        compiler_params=pltpu.CompilerParams(
            dimension_semantics=("arbitrary",), vmem_limit_bytes=VMEM_LIMIT),
        name="mix",
    )(attn, sga, pc, x, wao, wout, g, wr, br)


def _moe_kernel(h_ref, gates_ref, x1_ref, wg_ref, wu_ref, wd_ref, g_ref, out_ref, acc_ref):
    e = pl.program_id(1)

    @pl.when(e == 0)
    def _():
        acc_ref[...] = jnp.zeros(acc_ref.shape, F32)

    hb = h_ref[...]
    a = _dot(hb, wg_ref[0])
    act = a * _sigmoid(a) * _dot(hb, wu_ref[0])
    y = _dot(act.astype(BF16), wd_ref[0])
    lane = lax.broadcasted_iota(I32, gates_ref.shape, 1)
    ge = jnp.sum(jnp.where(lane == e, gates_ref[...], 0.0), axis=-1, keepdims=True)
    acc_ref[...] = acc_ref[...] + ge * y

    @pl.when(e == pl.num_programs(1) - 1)
    def _():
        out_ref[...] = _rms(x1_ref[...] + acc_ref[...], g_ref[...])


def _moe_call(hb, gates, x1, wg, wu, wd, g, tm):
    n, d = x1.shape
    ne, _, de = wg.shape
    tok = lambda c: pl.BlockSpec((tm, c), lambda i, e: (i, 0))
    return pl.pallas_call(
        _moe_kernel,
        grid=(n // tm, ne),
        in_specs=[tok(d), tok(LANES), tok(d),
                  pl.BlockSpec((1, d, de), lambda i, e: (e, 0, 0)),
                  pl.BlockSpec((1, d, de), lambda i, e: (e, 0, 0)),
                  pl.BlockSpec((1, de, d), lambda i, e: (e, 0, 0)),
                  pl.BlockSpec(g.shape, lambda i, e: (0, 0))],
        out_specs=tok(d),
        out_shape=jax.ShapeDtypeStruct((n, d), F32),
        scratch_shapes=[pltpu.VMEM((tm, d), F32)],
        compiler_params=pltpu.CompilerParams(
            dimension_semantics=("arbitrary", "arbitrary"), vmem_limit_bytes=VMEM_LIMIT),
        name="moe",
    )(hb, gates, x1, wg, wu, wd, g)


def _head_pair_perm():
    cols = []
    for g in range(GROUP):
        for n in range(N_KV_HEADS):
            h = n * GROUP + g
            cols.extend(range(h * HEAD_DIM, (h + 1) * HEAD_DIM))
    return np.asarray(cols, np.int32)


def _prep_weights(w_in, w_attn_out, w_conv_out, w_out, w_group, b_group, w_router, b_router,
                  w_gate, w_up, w_down):
    d = w_in.shape[0]
    o = np.cumsum([0, ATTN_DIM, KV_DIM, KV_DIM, IDX_HEADS * IDX_DIM, IDX_DIM, IDX_HEADS,
                   CONV_DIM, CONV_DIM, CONV_DIM, d, d])
    perm = _head_pair_perm()
    wq = w_in[:, o[0]:o[1]][:, perm] * ATTN_SCALE
    wa = jnp.concatenate([wq, w_in[:, o[1]:o[3]]], axis=1).astype(BF16)
    wki = w_in[:, o[4]:o[5]]
    wwi = jnp.pad(w_in[:, o[5]:o[6]] * IDX_W_SCALE, ((0, 0), (0, LANES - IDX_HEADS)))
    wb = jnp.concatenate([w_in[:, o[3]:o[4]] * IDX_SCALE, wki, wki, wwi], axis=1).astype(BF16)
    wc = w_in[:, o[6]:o[9]].astype(BF16)
    wg = w_in[:, o[9]:o[11]].astype(BF16)
    wr = jnp.concatenate([jnp.pad(w_group, ((0, 0), (0, LANES - N_GROUPS))),
                          jnp.pad(w_router, ((0, 0), (0, LANES - N_EXPERTS)))], axis=1).astype(BF16)
    br = jnp.stack([jnp.pad(b_group, (0, LANES - N_GROUPS)),
                    jnp.pad(b_router, (0, LANES - N_EXPERTS))]).astype(F32)
    return dict(wa=wa, wb=wb, wc=wc, wg=wg, wao=w_attn_out[perm, :].astype(BF16),
                wco=w_conv_out.astype(BF16), wout=w_out.astype(BF16), wr=wr, br=br,
                wgate=w_gate.astype(BF16), wup=w_up.astype(BF16), wdown=w_down.astype(BF16))


def _tile(n, pref):
    t = min(n, pref)
    assert n % t == 0, (n, pref)
    return t


def _layer(x, past, w, norm_mix, w_conv, norm_ffn, rel_bias):
    bsz, t, d = x.shape
    tk = KEY_TILE
    cinit = jnp.zeros((bsz, CONV_W - 1, CONV_DIM), F32) if past is None else past[3].astype(F32)
    (q, k, v, kb, vb, qi, kidx, kid, wi, sga, pc, cst) = _proj_call(
        x, cinit, norm_mix[None, :], w["wa"], w["wb"], w["wc"], w["wg"], w_conv, w["wco"], _tile(t, 256))

    if past is None:
        assert t % tk == 0
        tq, q_off, total = tk, 0, t
        n_valid_last = tk
        kid_all, kb_all, vb_all = kid, kb, vb
    else:
        p_len = past[0].shape[1]
        assert p_len % tk == 0 and t <= tk and t % 16 == 0
        tq, q_off, total = t, p_len, p_len + t
        n_valid_last = t
        pad = ((0, 0), (0, tk - t), (0, 0))
        pk = past[0].reshape(bsz, p_len, KV_DIM).astype(BF16)
        pv = past[1].reshape(bsz, p_len, KV_DIM).astype(BF16)
        pki = past[2].astype(BF16)
        kb_all = jnp.concatenate([pk, jnp.pad(kb, pad)], axis=1)
        vb_all = jnp.concatenate([pv, jnp.pad(vb, pad)], axis=1)
        kid_all = jnp.concatenate([jnp.concatenate([pki, pki], axis=-1), jnp.pad(kid, pad)], axis=1)
    nt = kb_all.shape[1] // tk
    tiles = lambda a: a.reshape(bsz, nt, tk, LANES)
    attn = _dsa_call(q, qi, wi, tiles(kid_all), tiles(kb_all), tiles(vb_all),
                     _bias_tables(rel_bias, tq, tk), tq=tq, q_off=q_off,
                     n_valid_last=n_valid_last, topk=min(TOPK_MAX, total // 4))

    n = bsz * t
    tm = _tile(n, 512)
    x1, h2, gates = _mix_call(attn.reshape(n, ATTN_DIM), sga.reshape(n, d), pc.reshape(n, d),
                              x.reshape(n, d), w["wao"], w["wout"], norm_ffn[None, :], w["wr"], w["br"], tm)
    state = (k.reshape(bsz, t, N_KV_HEADS, HEAD_DIM), v.reshape(bsz, t, N_KV_HEADS, HEAD_DIM), kidx, cst)
    return x1, h2, gates, state


def kernel(x_prompt, x_sample, cache_k, cache_v, cache_kidx, state_conv, rel_bias, norm_mix, w_in,
           w_attn_out, w_conv, w_conv_out, w_out, norm_ffn, w_group, b_group, w_router, b_router,
           w_gate, w_up, w_down, norm_final):
    depth = w_in.shape[0]
    assert depth == 1, "the final norm is fused into the last layer's expert kernel"
    l = 0
    w = _prep_weights(w_in[l], w_attn_out[l], w_conv_out[l], w_out[l], w_group[l], b_group[l],
                      w_router[l], b_router[l], w_gate[l], w_up[l], w_down[l])
    outs = []
    states = []
    for x, past in ((x_prompt, None),
                    (x_sample, (cache_k[l], cache_v[l], cache_kidx[l], state_conv[l]))):
        bsz, t, d = x.shape
        x1, h2, gates, state = _layer(x, past, w, norm_mix[l], w_conv[l], norm_ffn[l], rel_bias)
        n = bsz * t
        y = _moe_call(h2, gates, x1, w["wgate"], w["wup"], w["wdown"], norm_final[None, :], _tile(n, 512))
        outs.append(y.reshape(bsz, t, d))
        states.append(state)
    sp, ss = states
    return (outs[0], outs[1], sp[0][None], sp[1][None], sp[2][None], sp[3][None],
            ss[0][None], ss[1][None], ss[2][None], ss[3][None])
```

```python
import functools
import math

import jax
import jax.numpy as jnp
import numpy as np
from jax import lax
from jax.experimental import pallas as pl
from jax.experimental.pallas import tpu as pltpu

F32 = jnp.float32
BF16 = jnp.bfloat16
I32 = jnp.int32

CHUNK = 64
N_HEADS = 8
N_KV_HEADS = 2
GROUP = N_HEADS // N_KV_HEADS
HEAD_DIM = 64
ATTN_DIM = N_HEADS * HEAD_DIM
KV_DIM = N_KV_HEADS * HEAD_DIM
IDX_HEADS = 4
IDX_DIM = 64
TOPK_MAX = 256
CONV_DIM = 512
CONV_W = 3
N_BUCKETS = 32
MAX_DISTANCE = 128
N_GROUPS = 4
EXPERTS_PER_GROUP = 4
N_EXPERTS = N_GROUPS * EXPERTS_PER_GROUP
EPS = 1e-6
ATTN_SCALE = HEAD_DIM ** -0.5
IDX_SCALE = IDX_DIM ** -0.5
IDX_W_SCALE = IDX_HEADS ** -0.5

LANES = 128
SUBLANES = 8
KEY_TILE = 256
ROW_STRIP = 128
VMEM_LIMIT = 56 * 1024 * 1024
INT_MIN = -2 ** 31
IDX_BIG = 2 ** 30
F32_LOWEST = float(np.finfo(np.float32).min)
LOG2E = math.log2(math.e)


def _dot(a, b):
    return jnp.dot(a, b, preferred_element_type=F32)


def _dot_nt(a, b):
    return lax.dot_general(a, b, (((1,), (1,)), ((), ())), preferred_element_type=F32)


def _sigmoid(x):
    return 1.0 / (1.0 + jnp.exp(-x))


def _rms(x, g):
    return x * lax.rsqrt(jnp.mean(x * x, axis=-1, keepdims=True) + EPS) * g


def _proj_kernel(x_ref, cinit_ref, g_ref, wa_ref, wb_ref, wc_ref, wg_ref, wconv_ref, wco_ref,
                 q_ref, k_ref, v_ref, kb_ref, vb_ref, qi_ref, kidx_ref, kid_ref, wi_ref,
                 sga_ref, pc_ref, cst_ref, carry_ref, *, tm):
    j = pl.program_id(1)
    hb = _rms(x_ref[0], g_ref[...]).astype(BF16)

    a = _dot(hb, wa_ref[...])
    q_ref[0] = a[:, :ATTN_DIM].astype(BF16)
    k = a[:, ATTN_DIM:ATTN_DIM + KV_DIM]
    v = a[:, ATTN_DIM + KV_DIM:]
    k_ref[0] = k
    v_ref[0] = v
    kb_ref[0] = k.astype(BF16)
    vb_ref[0] = v.astype(BF16)

    b = _dot(hb, wb_ref[...])
    nqi = IDX_HEADS * IDX_DIM
    qi_ref[0] = b[:, :nqi].astype(BF16)
    kid_ref[0] = b[:, nqi:nqi + LANES].astype(BF16)
    kidx_ref[0] = b[:, nqi:nqi + IDX_DIM]
    wi_ref[0] = b[:, nqi + LANES:]

    c = _dot(hb, wc_ref[...])
    cb = c[:, :CONV_DIM]
    u = c[:, CONV_DIM:2 * CONV_DIM] * c[:, 2 * CONV_DIM:]

    @pl.when(j == 0)
    def _():
        carry_ref[6:8, :] = cinit_ref[0]

    c2 = carry_ref[6:7, :]
    c1 = carry_ref[7:8, :]
    row = lax.broadcasted_iota(I32, (tm, CONV_DIM), 0)
    up1 = jnp.where(row == 0, c1, pltpu.roll(u, 1, axis=0))
    up2 = jnp.where(row == 0, c2, jnp.where(row == 1, c1, pltpu.roll(u, 2, axis=0)))
    conv = wconv_ref[0:1, :] * up2 + wconv_ref[1:2, :] * up1 + wconv_ref[2:3, :] * u
    tail = u[tm - 2:tm, :]
    carry_ref[6:8, :] = tail
    cst_ref[0] = tail
    yc = _dot((cb * conv).astype(BF16), wco_ref[...])

    gt = _dot(hb, wg_ref[...])
    d = yc.shape[-1]
    sga_ref[0] = _sigmoid(gt[:, :d])
    pc_ref[0] = _sigmoid(gt[:, d:]) * yc


def _proj_call(x, cinit, g, wa, wb, wc, wg, wconv, wco, tm):
    bsz, s, d = x.shape
    grid = (bsz, s // tm)
    tok = lambda n: pl.BlockSpec((1, tm, n), lambda b, j: (b, j, 0))
    full = lambda a: pl.BlockSpec(a.shape, lambda b, j: (0,) * a.ndim)
    st = pl.BlockSpec((1, CONV_W - 1, CONV_DIM), lambda b, j: (b, 0, 0))
    out_shape = (
        jax.ShapeDtypeStruct((bsz, s, ATTN_DIM), BF16),
        jax.ShapeDtypeStruct((bsz, s, KV_DIM), F32),
        jax.ShapeDtypeStruct((bsz, s, KV_DIM), F32),
        jax.ShapeDtypeStruct((bsz, s, KV_DIM), BF16),
        jax.ShapeDtypeStruct((bsz, s, KV_DIM), BF16),
        jax.ShapeDtypeStruct((bsz, s, IDX_HEADS * IDX_DIM), BF16),
        jax.ShapeDtypeStruct((bsz, s, IDX_DIM), F32),
        jax.ShapeDtypeStruct((bsz, s, LANES), BF16),
        jax.ShapeDtypeStruct((bsz, s, LANES), F32),
        jax.ShapeDtypeStruct((bsz, s, d), F32),
        jax.ShapeDtypeStruct((bsz, s, d), F32),
        jax.ShapeDtypeStruct((bsz, CONV_W - 1, CONV_DIM), F32),
    )
    out_specs = (tok(ATTN_DIM), tok(KV_DIM), tok(KV_DIM), tok(KV_DIM), tok(KV_DIM),
                 tok(IDX_HEADS * IDX_DIM), tok(IDX_DIM), tok(LANES), tok(LANES), tok(d), tok(d), st)
    return pl.pallas_call(
        functools.partial(_proj_kernel, tm=tm),
        grid=grid,
        in_specs=[tok(d), st, full(g), full(wa), full(wb), full(wc), full(wg), full(wconv), full(wco)],
        out_specs=out_specs,
        out_shape=out_shape,
        scratch_shapes=[pltpu.VMEM((8, CONV_DIM), F32)],
        compiler_params=pltpu.CompilerParams(
            dimension_semantics=("arbitrary", "arbitrary"), vmem_limit_bytes=VMEM_LIMIT),
        name="proj",
    )(x, cinit, g, wa, wb, wc, wg, wconv, wco)


def _tree_sum(xs):
    while len(xs) > 1:
        xs = [xs[a] + xs[a + 1] for a in range(0, len(xs) - 1, 2)] + ([xs[-1]] if len(xs) % 2 else [])
    return xs[0]


def _dsa_kernel(q_ref, qi_ref, wit_ref, kid_ref, kb_ref, vb_ref, bt_ref, out_ref,
                keys_ref, qis_ref, qs_ref, eye_ref, jcut_ref, m_ref, l_ref, acc_ref,
                *, tq, tk, q_off, n_valid_last, topk, idx_bits):
    i = pl.program_id(1)
    nt = (q_off + i * tq) // tk + 1
    neg_inf = F32(-jnp.inf)

    lane = lax.broadcasted_iota(I32, (tq, LANES), 1)
    lo = lane < HEAD_DIM
    zero_b = jnp.zeros((tq, LANES), BF16)
    qi = qi_ref[0]
    for h in range(IDX_HEADS):
        blk = qi[:, (h // 2) * LANES:(h // 2 + 1) * LANES]
        qis_ref[h * tq:(h + 1) * tq, :] = jnp.where(lo if h % 2 == 0 else ~lo, blk, zero_b)
    q = q_ref[0]
    for n in range(N_KV_HEADS):
        for g in range(GROUP):
            blk = q[:, g * LANES:(g + 1) * LANES]
            qs_ref[n, g * tq:(g + 1) * tq, :] = jnp.where(lo if n == 0 else ~lo, blk, zero_b)
    eye_ref[...] = jnp.where(lax.broadcasted_iota(I32, (tq, tq), 0) == lax.broadcasted_iota(I32, (tq, tq), 1),
                             1.0, 0.0).astype(BF16)
    w_row = jnp.concatenate([wit_ref[0, h:h + 1, :] for h in range(IDX_HEADS)], axis=1)

    krow = lax.broadcasted_iota(I32, (tk, tq), 0)
    qcol = lax.broadcasted_iota(I32, (tk, tq), 1)
    lim_last = jnp.minimum(((qcol >> 6) + 1) << 6, n_valid_last)

    def score_body(t, carry):
        s = _dot_nt(kid_ref[0, t], qis_ref[...])
        s = jnp.maximum(s, 0.0) * w_row
        sc = (s[:, 0:tq] + s[:, tq:2 * tq]) + (s[:, 2 * tq:3 * tq] + s[:, 3 * tq:4 * tq])
        sc = jnp.where(sc == 0.0, 0.0, sc)
        bits = lax.bitcast_convert_type(sc, I32)
        key = bits ^ ((bits >> 31) & 0x7FFFFFFF)
        lim = jnp.where(t == nt - 1, lim_last, tk)
        keys_ref[t] = jnp.where(krow < lim, key, INT_MIN)
        return carry

    lax.fori_loop(0, nt, score_body, 0)

    def count(ind_fn, cand):
        cand8 = jnp.broadcast_to(cand, (SUBLANES, tq))
        row8 = lax.broadcasted_iota(I32, (SUBLANES, tq), 0)

        def body(t, acc):
            parts = []
            for r in range(tk // SUBLANES):
                x = keys_ref[t, r * SUBLANES:(r + 1) * SUBLANES, :]
                parts.append(ind_fn(x, row8 + (t * tk + r * SUBLANES), cand8))
            return acc + _tree_sum(parts)

        acc = lax.fori_loop(0, nt, body, jnp.zeros((SUBLANES, tq), I32))
        return jnp.sum(acc.astype(F32), axis=0, keepdims=True)

    ge = lambda x, idx, c: jnp.where(x >= c, 1, 0)

    def bit_body(b, thr):
        cand = thr + (jnp.int32(1) << (31 - b))
        return jnp.where(count(ge, cand) >= topk, cand, thr)

    thr = lax.fori_loop(0, 32, bit_body, jnp.full((1, tq), INT_MIN, I32))
    thr8 = jnp.broadcast_to(thr, (SUBLANES, tq))

    jcut_ref[...] = jnp.full(jcut_ref.shape, IDX_BIG, I32)
    n_ge = count(ge, thr)

    @pl.when(jnp.max(n_ge) > topk)
    def _():
        need = topk - count(ge, thr + 1)
        tied_lt = lambda x, idx, c: jnp.where(x == thr8, jnp.where(idx < c, 1, 0), 0)

        def jbit_body(b, jj):
            cand = jj + (jnp.int32(1) << (idx_bits - 1 - b))
            return jnp.where(count(tied_lt, cand) <= need, cand, jj)

        jj = lax.fori_loop(0, idx_bits, jbit_body, jnp.zeros((1, tq), I32))
        jcut_ref[...] = jnp.broadcast_to(jj, jcut_ref.shape)

    jcut = jnp.where(thr == INT_MIN, 0, jcut_ref[0:1, :])

    m_ref[...] = jnp.full(m_ref.shape, neg_inf, F32)
    l_ref[...] = jnp.zeros(l_ref.shape, F32)
    acc_ref[...] = jnp.zeros(acc_ref.shape, F32)
    n_col = tk // LANES

    def tile(t, near):
        key = keys_ref[t]
        sel_t = jnp.where(key > thr, 1.0, jnp.where(key == thr, jnp.where(krow + t * tk < jcut, 1.0, 0.0), 0.0))
        sel = _dot_nt(eye_ref[...], sel_t.astype(BF16))
        selb = jnp.where(sel > 0.5, 0.0, neg_inf)
        kt = kb_ref[0, t]
        vt = vb_ref[0, t]
        n_strip = tq // ROW_STRIP
        for n in range(N_KV_HEADS):
            lg_all = _dot_nt(qs_ref[n], kt)
            p_all, alphas = [], []
            for s in range(GROUP * n_strip):
                rows = slice(s * ROW_STRIP, (s + 1) * ROW_STRIP)
                hf = s % n_strip
                lg = lg_all[rows, :] + selb[hf * ROW_STRIP:(hf + 1) * ROW_STRIP, :]
                if near is not None:
                    lg = lg + bt_ref[near, n, rows, :]
                cols = [lg[:, c * LANES:(c + 1) * LANES] for c in range(n_col)]
                mx = cols[0]
                for c in range(1, n_col):
                    mx = jnp.maximum(mx, cols[c])
                m_old = m_ref[n, rows, :]
                m_new = jnp.maximum(m_old, jnp.max(mx, axis=-1, keepdims=True))
                m_safe = jnp.maximum(m_new, F32_LOWEST)
                alpha = jnp.exp2(m_old - m_safe)
                ps = [jnp.exp2(c - m_safe) for c in cols]
                m_ref[n, rows, :] = m_new
                l_ref[n, rows, :] = alpha * l_ref[n, rows, :] + _tree_sum(list(ps))
                p_all.append(jnp.concatenate(ps, axis=1).astype(BF16))
                alphas.append(alpha)
            pv_all = _dot(jnp.concatenate(p_all, axis=0), vt)
            for s in range(GROUP * n_strip):
                rows = slice(s * ROW_STRIP, (s + 1) * ROW_STRIP)
                acc_ref[n, rows, :] = alphas[s] * acc_ref[n, rows, :] + pv_all[rows, :]

    def far_body(t, carry):
        tile(t, None)
        return carry

    lax.fori_loop(0, nt - 2, far_body, 0)

    @pl.when(nt >= 2)
    def _():
        tile(nt - 2, 0)

    tile(nt - 1, 1)

    o = []
    for n in range(N_KV_HEADS):
        o.append(acc_ref[n] / jnp.sum(l_ref[n], axis=-1, keepdims=True))
    for g in range(GROUP):
        blk = jnp.where(lo, o[0][g * tq:(g + 1) * tq, :], o[1][g * tq:(g + 1) * tq, :])
        out_ref[0, :, g * LANES:(g + 1) * LANES] = blk.astype(BF16)


def _dsa_call(q, qi, wit, kid, kb, vb, bt, *, tq, q_off, n_valid_last, topk):
    bsz, sq, _ = q.shape
    nt_max, tk = kid.shape[1], kid.shape[2]
    idx_bits = int(nt_max * tk).bit_length()
    r4 = GROUP * tq
    tok = lambda n: pl.BlockSpec((1, tq, n), lambda b, i: (b, i, 0))
    keys = pl.BlockSpec((1, nt_max, tk, LANES), lambda b, i: (b, 0, 0, 0))
    kern = functools.partial(_dsa_kernel, tq=tq, tk=tk, q_off=q_off, n_valid_last=n_valid_last,
                             topk=topk, idx_bits=idx_bits)
    return pl.pallas_call(
        kern,
        grid=(bsz, sq // tq),
        in_specs=[tok(ATTN_DIM), tok(IDX_HEADS * IDX_DIM),
                  pl.BlockSpec((1, SUBLANES, tq), lambda b, i: (b, 0, i)), keys, keys, keys,
                  pl.BlockSpec(bt.shape, lambda b, i: (0, 0, 0, 0))],
        out_specs=tok(ATTN_DIM),
        out_shape=jax.ShapeDtypeStruct((bsz, sq, ATTN_DIM), BF16),
        scratch_shapes=[
            pltpu.VMEM((nt_max, tk, tq), I32),
            pltpu.VMEM((IDX_HEADS * tq, LANES), BF16),
            pltpu.VMEM((N_KV_HEADS, r4, LANES), BF16),
            pltpu.VMEM((tq, tq), BF16),
            pltpu.VMEM((SUBLANES, tq), I32),
            pltpu.VMEM((N_KV_HEADS, r4, LANES), F32),
            pltpu.VMEM((N_KV_HEADS, r4, LANES), F32),
            pltpu.VMEM((N_KV_HEADS, r4, LANES), F32),
        ],
        compiler_params=pltpu.CompilerParams(
            dimension_semantics=("arbitrary", "arbitrary"), vmem_limit_bytes=VMEM_LIMIT),
        name="dsa",
    )(q, qi, wit, kid, kb, vb, bt)


def _t5_bucket(rel):
    nb = N_BUCKETS // 2
    max_exact = nb // 2
    ret = jnp.where(rel > 0, nb, 0)
    n = jnp.abs(rel)
    nf = jnp.maximum(n, 1).astype(F32)
    large = max_exact + (jnp.log(nf / max_exact) / math.log(MAX_DISTANCE / max_exact)
                         * (nb - max_exact)).astype(I32)
    large = jnp.minimum(large, nb - 1)
    return ret + jnp.where(n < max_exact, n, large)


def _bias_tables(rel_bias, tq, tk):
    far = rel_bias[_t5_bucket(jnp.full((1,), -(tk + 1), I32))[0]]
    period = 2 * tk + tq
    off = jnp.arange(period, dtype=I32)
    off = jnp.where(off < 2 * tk, off, off - period)
    diag = ((rel_bias[_t5_bucket(off - tk)] - far) * LOG2E).T
    tab = jnp.tile(diag, (1, tq))[:, :tq * (period - 1)].reshape(N_HEADS, tq, period - 1)[:, :, :2 * tk]
    tab = tab.reshape(N_KV_HEADS, GROUP, tq, 2, tk).transpose(3, 0, 1, 2, 4)
    return tab.reshape(2, N_KV_HEADS, GROUP * tq, tk).astype(F32)


def _mix_kernel(attn_ref, sga_ref, pc_ref, x_ref, wao_ref, wout_ref, g_ref, wr_ref, br_ref,
                x1_ref, h2_ref, gates_ref):
    ya = _dot(attn_ref[...], wao_ref[...])
    merged = sga_ref[...] * ya + pc_ref[...]
    x1 = x_ref[...] + _dot(merged.astype(BF16), wout_ref[...])
    x1_ref[...] = x1
    hb = _rms(x1, g_ref[...]).astype(BF16)
    h2_ref[...] = hb

    rl = _dot(hb, wr_ref[...])
    tm = rl.shape[0]
    lane = lax.broadcasted_iota(I32, (tm, LANES), 1)
    neg_inf = F32(-jnp.inf)
    big = I32(LANES)

    def first_argmax(x):
        mx = jnp.max(x, axis=-1, keepdims=True)
        return jnp.min(jnp.where(x == mx, lane, big), axis=-1, keepdims=True)

    def pick(x, idx):
        return jnp.sum(jnp.where(lane == idx, x, 0.0), axis=-1, keepdims=True)

    glog = rl[:, :LANES]
    gvalid = lane < N_GROUPS
    g_sel = first_argmax(jnp.where(gvalid, glog + br_ref[0:1, :], neg_inf))
    gm = jnp.max(jnp.where(gvalid, glog, neg_inf), axis=-1, keepdims=True)
    gexp = jnp.where(gvalid, jnp.exp(glog - gm), 0.0)
    g_prob = pick(gexp, g_sel) / jnp.sum(gexp, axis=-1, keepdims=True)

    elog = rl[:, LANES:]
    in_grp = (lane // EXPERTS_PER_GROUP) == g_sel
    eb = jnp.where(in_grp, elog + br_ref[1:2, :], neg_inf)
    i1 = first_argmax(eb)
    i2 = first_argmax(jnp.where(lane == i1, neg_inf, eb))
    em = jnp.max(jnp.where(in_grp, elog, neg_inf), axis=-1, keepdims=True)
    eexp = jnp.where(in_grp, jnp.exp(elog - em), 0.0)
    esum = jnp.sum(eexp, axis=-1, keepdims=True)
    p1 = pick(eexp, i1) / esum
    p2 = pick(eexp, i2) / esum
    w1 = g_prob * p1 / (p1 + p2)
    w2 = g_prob * p2 / (p1 + p2)
    gates_ref[...] = jnp.where(lane == i1, w1, 0.0) + jnp.where(lane == i2, w2, 0.0)


def _mix_call(attn, sga, pc, x, wao, wout, g, wr, br, tm):
    n, d = x.shape
    tok = lambda c: pl.BlockSpec((tm, c), lambda i: (i, 0))
    full = lambda a: pl.BlockSpec(a.shape, lambda i: (0,) * a.ndim)
    return pl.pallas_call(
        _mix_kernel,
        grid=(n // tm,),
        in_specs=[tok(ATTN_DIM), tok(d), tok(d), tok(d), full(wao), full(wout), full(g), full(wr), full(br)],
        out_specs=(tok(d), tok(d), tok(LANES)),
        out_shape=(jax.ShapeDtypeStruct((n, d), F32), jax.ShapeDtypeStruct((n, d), BF16),
                   jax.ShapeDtypeStruct((n, LANES), F32)),
        compiler_params=pltpu.CompilerParams(
            dimension_semantics=("arbitrary",), vmem_limit_bytes=VMEM_LIMIT),
        name="mix",
    )(attn, sga, pc, x, wao, wout, g, wr, br)


def _moe_kernel(h_ref, gates_ref, x1_ref, wg_ref, wu_ref, wd_ref, g_ref, out_ref, acc_ref):
    e = pl.program_id(1)

    @pl.when(e == 0)
    def _():
        acc_ref[...] = jnp.zeros(acc_ref.shape, F32)

    hb = h_ref[...]
    a = _dot(hb, wg_ref[0])
    act = a * _sigmoid(a) * _dot(hb, wu_ref[0])
    y = _dot(act.astype(BF16), wd_ref[0])
    lane = lax.broadcasted_iota(I32, gates_ref.shape, 1)
    ge = jnp.sum(jnp.where(lane == e, gates_ref[...], 0.0), axis=-1, keepdims=True)
    acc_ref[...] = acc_ref[...] + ge * y

    @pl.when(e == pl.num_programs(1) - 1)
    def _():
        out_ref[...] = _rms(x1_ref[...] + acc_ref[...], g_ref[...])


def _moe_call(hb, gates, x1, wg, wu, wd, g, tm):
    n, d = x1.shape
    ne, _, de = wg.shape
    tok = lambda c: pl.BlockSpec((tm, c), lambda i, e: (i, 0))
    return pl.pallas_call(
        _moe_kernel,
        grid=(n // tm, ne),
        in_specs=[tok(d), tok(LANES), tok(d),
                  pl.BlockSpec((1, d, de), lambda i, e: (e, 0, 0)),
                  pl.BlockSpec((1, d, de), lambda i, e: (e, 0, 0)),
                  pl.BlockSpec((1, de, d), lambda i, e: (e, 0, 0)),
                  pl.BlockSpec(g.shape, lambda i, e: (0, 0))],
        out_specs=tok(d),
        out_shape=jax.ShapeDtypeStruct((n, d), F32),
        scratch_shapes=[pltpu.VMEM((tm, d), F32)],
        compiler_params=pltpu.CompilerParams(
            dimension_semantics=("arbitrary", "arbitrary"), vmem_limit_bytes=VMEM_LIMIT),
        name="moe",
    )(hb, gates, x1, wg, wu, wd, g)


def _head_pair_perm():
    cols = []
    for g in range(GROUP):
        for n in range(N_KV_HEADS):
            h = n * GROUP + g
            cols.extend(range(h * HEAD_DIM, (h + 1) * HEAD_DIM))
    return np.asarray(cols, np.int32)


def _prep_weights(w_in, w_attn_out, w_conv_out, w_out, w_group, b_group, w_router, b_router,
                  w_gate, w_up, w_down):
    d = w_in.shape[0]
    o = np.cumsum([0, ATTN_DIM, KV_DIM, KV_DIM, IDX_HEADS * IDX_DIM, IDX_DIM, IDX_HEADS,
                   CONV_DIM, CONV_DIM, CONV_DIM, d, d])
    perm = _head_pair_perm()
    wq = w_in[:, o[0]:o[1]][:, perm] * (ATTN_SCALE * LOG2E)
    wa = jnp.concatenate([wq, w_in[:, o[1]:o[3]]], axis=1).astype(BF16)
    wki = w_in[:, o[4]:o[5]]
    wwi = jnp.pad(w_in[:, o[5]:o[6]] * IDX_W_SCALE, ((0, 0), (0, LANES - IDX_HEADS)))
    wb = jnp.concatenate([w_in[:, o[3]:o[4]] * IDX_SCALE, wki, wki, wwi], axis=1).astype(BF16)
    wc = w_in[:, o[6]:o[9]].astype(BF16)
    wg = w_in[:, o[9]:o[11]].astype(BF16)
    wr = jnp.concatenate([jnp.pad(w_group, ((0, 0), (0, LANES - N_GROUPS))),
                          jnp.pad(w_router, ((0, 0), (0, LANES - N_EXPERTS)))], axis=1).astype(BF16)
    br = jnp.stack([jnp.pad(b_group, (0, LANES - N_GROUPS)),
                    jnp.pad(b_router, (0, LANES - N_EXPERTS))]).astype(F32)
    return dict(wa=wa, wb=wb, wc=wc, wg=wg, wao=w_attn_out[perm, :].astype(BF16),
                wco=w_conv_out.astype(BF16), wout=w_out.astype(BF16), wr=wr, br=br,
                wgate=w_gate.astype(BF16), wup=w_up.astype(BF16), wdown=w_down.astype(BF16))


def _tile(n, pref):
    t = min(n, pref)
    assert n % t == 0, (n, pref)
    return t


def _layer(x, past, w, norm_mix, w_conv, norm_ffn, rel_bias):
    bsz, t, d = x.shape
    tk = KEY_TILE
    cinit = jnp.zeros((bsz, CONV_W - 1, CONV_DIM), F32) if past is None else past[3].astype(F32)
    (q, k, v, kb, vb, qi, kidx, kid, wi, sga, pc, cst) = _proj_call(
        x, cinit, norm_mix[None, :], w["wa"], w["wb"], w["wc"], w["wg"], w_conv, w["wco"], _tile(t, 256))

    if past is None:
        assert t % tk == 0
        tq, q_off, total = tk, 0, t
        n_valid_last = tk
        kid_all, kb_all, vb_all = kid, kb, vb
    else:
        p_len = past[0].shape[1]
        tq = ROW_STRIP
        assert p_len % tk == 0 and t <= tq
        q_off, total = p_len, p_len + t
        n_valid_last = t
        qpad = ((0, 0), (0, tq - t), (0, 0))
        q, qi, wi = jnp.pad(q, qpad), jnp.pad(qi, qpad), jnp.pad(wi, qpad)
        pad = ((0, 0), (0, tk - t), (0, 0))
        pk = past[0].reshape(bsz, p_len, KV_DIM).astype(BF16)
        pv = past[1].reshape(bsz, p_len, KV_DIM).astype(BF16)
        pki = past[2].astype(BF16)
        kb_all = jnp.concatenate([pk, jnp.pad(kb, pad)], axis=1)
        vb_all = jnp.concatenate([pv, jnp.pad(vb, pad)], axis=1)
        kid_all = jnp.concatenate([jnp.concatenate([pki, pki], axis=-1), jnp.pad(kid, pad)], axis=1)
    nt = kb_all.shape[1] // tk
    tiles = lambda a: a.reshape(bsz, nt, tk, LANES)
    wit = jnp.swapaxes(wi[:, :, :SUBLANES], 1, 2)
    attn = _dsa_call(q, qi, wit, tiles(kid_all), tiles(kb_all), tiles(vb_all),
                     _bias_tables(rel_bias, tq, tk), tq=tq, q_off=q_off,
                     n_valid_last=n_valid_last, topk=min(TOPK_MAX, total // 4))
    attn = attn[:, :t]

    n = bsz * t
    tm = _tile(n, 512)
    x1, h2, gates = _mix_call(attn.reshape(n, ATTN_DIM), sga.reshape(n, d), pc.reshape(n, d),
                              x.reshape(n, d), w["wao"], w["wout"], norm_ffn[None, :], w["wr"], w["br"], tm)
    state = (k.reshape(bsz, t, N_KV_HEADS, HEAD_DIM), v.reshape(bsz, t, N_KV_HEADS, HEAD_DIM), kidx, cst)
    return x1, h2, gates, state


def kernel(x_prompt, x_sample, cache_k, cache_v, cache_kidx, state_conv, rel_bias, norm_mix, w_in,
           w_attn_out, w_conv, w_conv_out, w_out, norm_ffn, w_group, b_group, w_router, b_router,
           w_gate, w_up, w_down, norm_final):
    depth = w_in.shape[0]
    assert depth == 1, "the final norm is fused into the last layer's expert kernel"
    l = 0
    w = _prep_weights(w_in[l], w_attn_out[l], w_conv_out[l], w_out[l], w_group[l], b_group[l],
                      w_router[l], b_router[l], w_gate[l], w_up[l], w_down[l])
    outs = []
    states = []
    for x, past in ((x_prompt, None),
                    (x_sample, (cache_k[l], cache_v[l], cache_kidx[l], state_conv[l]))):
        bsz, t, d = x.shape
        x1, h2, gates, state = _layer(x, past, w, norm_mix[l], w_conv[l], norm_ffn[l], rel_bias)
        n = bsz * t
        y = _moe_call(h2, gates, x1, w["wgate"], w["wup"], w["wdown"], norm_final[None, :], _tile(n, 512))
        outs.append(y.reshape(bsz, t, d))
        states.append(state)
    sp, ss = states
    return (outs[0], outs[1], sp[0][None], sp[1][None], sp[2][None], sp[3][None],
            ss[0][None], ss[1][None], ss[2][None], ss[3][None])
```

```python
import functools
import math

import jax
import jax.numpy as jnp
import numpy as np
from jax import lax
from jax.experimental import pallas as pl
from jax.experimental.pallas import tpu as pltpu

F32 = jnp.float32
BF16 = jnp.bfloat16
I32 = jnp.int32

CHUNK = 64
N_HEADS = 8
N_KV_HEADS = 2
GROUP = N_HEADS // N_KV_HEADS
HEAD_DIM = 64
ATTN_DIM = N_HEADS * HEAD_DIM
KV_DIM = N_KV_HEADS * HEAD_DIM
IDX_HEADS = 4
IDX_DIM = 64
TOPK_MAX = 256
CONV_DIM = 512
CONV_W = 3
N_BUCKETS = 32
MAX_DISTANCE = 128
N_GROUPS = 4
EXPERTS_PER_GROUP = 4
N_EXPERTS = N_GROUPS * EXPERTS_PER_GROUP
EPS = 1e-6
ATTN_SCALE = HEAD_DIM ** -0.5
IDX_SCALE = IDX_DIM ** -0.5
IDX_W_SCALE = IDX_HEADS ** -0.5

LANES = 128
SUBLANES = 8
KEY_TILE = 256
ROW_STRIP = 128
VMEM_LIMIT = 56 * 1024 * 1024
INT_MIN = -2 ** 31
IDX_BIG = 2 ** 30
F32_LOWEST = float(np.finfo(np.float32).min)
LOG2E = math.log2(math.e)
BOUND_SLACK = 1.0 + 2.0 ** -5
L_UNDERFLOW = 2.0 ** -80


def _dot(a, b):
    return jnp.dot(a, b, preferred_element_type=F32)


def _dot_nt(a, b):
    return lax.dot_general(a, b, (((1,), (1,)), ((), ())), preferred_element_type=F32)


def _sigmoid(x):
    return 1.0 / (1.0 + jnp.exp(-x))


def _rms(x, g):
    return x * lax.rsqrt(jnp.mean(x * x, axis=-1, keepdims=True) + EPS) * g


def _proj_kernel(x_ref, cinit_ref, g_ref, wa_ref, wb_ref, wc_ref, wg_ref, wconv_ref, wco_ref,
                 q_ref, k_ref, v_ref, kb_ref, vb_ref, qi_ref, kidx_ref, kid_ref, wi_ref,
                 sga_ref, pc_ref, cst_ref, carry_ref, *, tm):
    j = pl.program_id(1)
    hb = _rms(x_ref[0], g_ref[...]).astype(BF16)

    a = _dot(hb, wa_ref[...])
    q_ref[0] = a[:, :ATTN_DIM].astype(BF16)
    k = a[:, ATTN_DIM:ATTN_DIM + KV_DIM]
    v = a[:, ATTN_DIM + KV_DIM:]
    k_ref[0] = k
    v_ref[0] = v
    kb_ref[0] = k.astype(BF16)
    vb_ref[0] = v.astype(BF16)

    b = _dot(hb, wb_ref[...])
    nqi = IDX_HEADS * IDX_DIM
    qi_ref[0] = b[:, :nqi].astype(BF16)
    kid_ref[0] = b[:, nqi:nqi + LANES].astype(BF16)
    kidx_ref[0] = b[:, nqi:nqi + IDX_DIM]
    wi_ref[0] = b[:, nqi + LANES:]

    c = _dot(hb, wc_ref[...])
    cb = c[:, :CONV_DIM]
    u = c[:, CONV_DIM:2 * CONV_DIM] * c[:, 2 * CONV_DIM:]

    @pl.when(j == 0)
    def _():
        carry_ref[6:8, :] = cinit_ref[0]

    c2 = carry_ref[6:7, :]
    c1 = carry_ref[7:8, :]
    row = lax.broadcasted_iota(I32, (tm, CONV_DIM), 0)
    up1 = jnp.where(row == 0, c1, pltpu.roll(u, 1, axis=0))
    up2 = jnp.where(row == 0, c2, jnp.where(row == 1, c1, pltpu.roll(u, 2, axis=0)))
    conv = wconv_ref[0:1, :] * up2 + wconv_ref[1:2, :] * up1 + wconv_ref[2:3, :] * u
    tail = u[tm - 2:tm, :]
    carry_ref[6:8, :] = tail
    cst_ref[0] = tail
    yc = _dot((cb * conv).astype(BF16), wco_ref[...])

    gt = _dot(hb, wg_ref[...])
    d = yc.shape[-1]
    sga_ref[0] = _sigmoid(gt[:, :d])
    pc_ref[0] = _sigmoid(gt[:, d:]) * yc


def _proj_call(x, cinit, g, wa, wb, wc, wg, wconv, wco, tm):
    bsz, s, d = x.shape
    grid = (bsz, s // tm)
    tok = lambda n: pl.BlockSpec((1, tm, n), lambda b, j: (b, j, 0))
    full = lambda a: pl.BlockSpec(a.shape, lambda b, j: (0,) * a.ndim)
    st = pl.BlockSpec((1, CONV_W - 1, CONV_DIM), lambda b, j: (b, 0, 0))
    out_shape = (
        jax.ShapeDtypeStruct((bsz, s, ATTN_DIM), BF16),
        jax.ShapeDtypeStruct((bsz, s, KV_DIM), F32),
        jax.ShapeDtypeStruct((bsz, s, KV_DIM), F32),
        jax.ShapeDtypeStruct((bsz, s, KV_DIM), BF16),
        jax.ShapeDtypeStruct((bsz, s, KV_DIM), BF16),
        jax.ShapeDtypeStruct((bsz, s, IDX_HEADS * IDX_DIM), BF16),
        jax.ShapeDtypeStruct((bsz, s, IDX_DIM), F32),
        jax.ShapeDtypeStruct((bsz, s, LANES), BF16),
        jax.ShapeDtypeStruct((bsz, s, LANES), F32),
        jax.ShapeDtypeStruct((bsz, s, d), F32),
        jax.ShapeDtypeStruct((bsz, s, d), F32),
        jax.ShapeDtypeStruct((bsz, CONV_W - 1, CONV_DIM), F32),
    )
    out_specs = (tok(ATTN_DIM), tok(KV_DIM), tok(KV_DIM), tok(KV_DIM), tok(KV_DIM),
                 tok(IDX_HEADS * IDX_DIM), tok(IDX_DIM), tok(LANES), tok(LANES), tok(d), tok(d), st)
    return pl.pallas_call(
        functools.partial(_proj_kernel, tm=tm),
        grid=grid,
        in_specs=[tok(d), st, full(g), full(wa), full(wb), full(wc), full(wg), full(wconv), full(wco)],
        out_specs=out_specs,
        out_shape=out_shape,
        scratch_shapes=[pltpu.VMEM((8, CONV_DIM), F32)],
        compiler_params=pltpu.CompilerParams(
            dimension_semantics=("arbitrary", "arbitrary"), vmem_limit_bytes=VMEM_LIMIT),
        name="proj",
    )(x, cinit, g, wa, wb, wc, wg, wconv, wco)


def _tree_sum(xs):
    while len(xs) > 1:
        xs = [xs[a] + xs[a + 1] for a in range(0, len(xs) - 1, 2)] + ([xs[-1]] if len(xs) % 2 else [])
    return xs[0]


def _dsa_kernel(q_ref, qi_ref, wit_ref, kid_ref, kb_ref, vbt_ref, bt_ref, btmax_ref, out_ref,
                keys_ref, qis_ref, qs_ref, eye_ref, jcut_ref, kmax_ref, m_ref, l_ref, acc_ref,
                *, tq, tk, q_off, n_valid_last, topk, idx_bits):
    i = pl.program_id(1)
    nt = (q_off + i * tq) // tk + 1
    neg_inf = F32(-jnp.inf)

    lane = lax.broadcasted_iota(I32, (tq, LANES), 1)
    lo = lane < HEAD_DIM
    zero_b = jnp.zeros((tq, LANES), BF16)
    qi = qi_ref[0]
    for h in range(IDX_HEADS):
        blk = qi[:, (h // 2) * LANES:(h // 2 + 1) * LANES]
        qis_ref[h * tq:(h + 1) * tq, :] = jnp.where(lo if h % 2 == 0 else ~lo, blk, zero_b)
    q = q_ref[0]
    for n in range(N_KV_HEADS):
        for g in range(GROUP):
            blk = q[:, g * LANES:(g + 1) * LANES]
            qs_ref[n, g * tq:(g + 1) * tq, :] = jnp.where(lo if n == 0 else ~lo, blk, zero_b)
    eye_ref[...] = jnp.where(lax.broadcasted_iota(I32, (tq, tq), 0) == lax.broadcasted_iota(I32, (tq, tq), 1),
                             1.0, 0.0).astype(BF16)
    w_row = jnp.concatenate([wit_ref[0, h:h + 1, :] for h in range(IDX_HEADS)], axis=1)

    krow = lax.broadcasted_iota(I32, (tk, tq), 0)
    qcol = lax.broadcasted_iota(I32, (tk, tq), 1)
    lim_last = jnp.minimum(((qcol >> 6) + 1) << 6, n_valid_last)

    def score_body(t, carry):
        s = _dot_nt(kid_ref[0, t], qis_ref[...])
        s = jnp.maximum(s, 0.0) * w_row
        sc = (s[:, 0:tq] + s[:, tq:2 * tq]) + (s[:, 2 * tq:3 * tq] + s[:, 3 * tq:4 * tq])
        sc = jnp.where(sc == 0.0, 0.0, sc)
        bits = lax.bitcast_convert_type(sc, I32)
        key = bits ^ ((bits >> 31) & 0x7FFFFFFF)
        lim = jnp.where(t == nt - 1, lim_last, tk)
        keys_ref[t] = jnp.where(krow < lim, key, INT_MIN)
        return carry

    lax.fori_loop(0, nt, score_body, 0)

    def count(ind_fn, cand):
        cand8 = jnp.broadcast_to(cand, (SUBLANES, tq))
        row8 = lax.broadcasted_iota(I32, (SUBLANES, tq), 0)

        def body(t, acc):
            parts = []
            for r in range(tk // SUBLANES):
                x = keys_ref[t, r * SUBLANES:(r + 1) * SUBLANES, :]
                parts.append(ind_fn(x, row8 + (t * tk + r * SUBLANES), cand8))
            return acc + _tree_sum(parts)

        acc = lax.fori_loop(0, nt, body, jnp.zeros((SUBLANES, tq), I32))
        return jnp.sum(acc.astype(F32), axis=0, keepdims=True)

    ge = lambda x, idx, c: jnp.where(x >= c, 1, 0)

    def bit_body(b, thr):
        cand = thr + (jnp.int32(1) << (31 - b))
        return jnp.where(count(ge, cand) >= topk, cand, thr)

    thr = lax.fori_loop(0, 32, bit_body, jnp.full((1, tq), INT_MIN, I32))
    thr8 = jnp.broadcast_to(thr, (SUBLANES, tq))

    jcut_ref[...] = jnp.full(jcut_ref.shape, IDX_BIG, I32)
    n_ge = count(ge, thr)

    @pl.when(jnp.max(n_ge) > topk)
    def _():
        need = topk - count(ge, thr + 1)
        tied_lt = lambda x, idx, c: jnp.where(x == thr8, jnp.where(idx < c, 1, 0), 0)

        def jbit_body(b, jj):
            cand = jj + (jnp.int32(1) << (idx_bits - 1 - b))
            return jnp.where(count(tied_lt, cand) <= need, cand, jj)

        jj = lax.fori_loop(0, idx_bits, jbit_body, jnp.zeros((1, tq), I32))
        jcut_ref[...] = jnp.broadcast_to(jj, jcut_ref.shape)

    jcut = jnp.where(thr == INT_MIN, 0, jcut_ref[0:1, :])

    @pl.when(i == 0)
    def _():
        half0 = lax.broadcasted_iota(I32, (tk, LANES), 1) < HEAD_DIM

        def kbody(t, c):
            k = kb_ref[0, t].astype(F32)
            sq = k * k
            s0 = jnp.max(jnp.sum(jnp.where(half0, sq, 0.0), axis=1, keepdims=True), axis=0, keepdims=True)
            s1 = jnp.max(jnp.sum(jnp.where(half0, 0.0, sq), axis=1, keepdims=True), axis=0, keepdims=True)
            return jnp.maximum(c[0], s0), jnp.maximum(c[1], s1)

        z = jnp.zeros((1, 1), F32)
        c = lax.fori_loop(0, kb_ref.shape[1], kbody, (z, z))
        for n in range(N_KV_HEADS):
            kmax_ref[n:n + 1, :] = jnp.broadcast_to(jnp.sqrt(c[n]), (1, LANES))

    ones8 = jnp.ones((SUBLANES, LANES), BF16)
    mb = []
    for n in range(N_KV_HEADS):
        qf = qs_ref[n].astype(F32)
        qsq = _dot_nt(ones8, (qf * qf).astype(BF16))[0:1, :]
        mb.append(BOUND_SLACK * jnp.sqrt(qsq) * kmax_ref[n:n + 1, 0:1] + btmax_ref[n, 0:1, :])

    def fold8(x, op):
        parts = [x[r * SUBLANES:(r + 1) * SUBLANES, :] for r in range(x.shape[0] // SUBLANES)]
        while len(parts) > 1:
            parts = [op(parts[a], parts[a + 1]) for a in range(0, len(parts), 2)]
        return parts[0]

    def logits(t, near, n, sel4):
        lg = _dot_nt(kb_ref[0, t], qs_ref[n]) + sel4
        if near is not None:
            lg = lg + bt_ref[near, n]
        return lg

    def selection(t):
        key = keys_ref[t]
        sel = jnp.where(key > thr, 0.0,
                        jnp.where(key == thr, jnp.where(krow + t * tk < jcut, 0.0, neg_inf), neg_inf))
        return jnp.concatenate([sel] * GROUP, axis=1)

    def tile_fast(t, near):
        sel4 = selection(t)
        vt = vbt_ref[0, t]
        for n in range(N_KV_HEADS):
            p = jnp.exp2(logits(t, near, n, sel4) - mb[n])
            l_ref[n] = l_ref[n] + fold8(p, jnp.add)
            acc_ref[n] = acc_ref[n] + _dot(vt[n * HEAD_DIM:(n + 1) * HEAD_DIM, :], p.astype(BF16))

    def tile_exact(t, near):
        sel4 = selection(t)
        vt = vbt_ref[0, t]
        for n in range(N_KV_HEADS):
            lg = logits(t, near, n, sel4)
            m_old = m_ref[n, 0:1, :]
            m_new = jnp.maximum(m_old, jnp.max(fold8(lg, jnp.maximum), axis=0, keepdims=True))
            m_safe = jnp.maximum(m_new, F32_LOWEST)
            alpha = jnp.exp2(m_old - m_safe)
            p = jnp.exp2(lg - m_safe)
            m_ref[n] = jnp.broadcast_to(m_new, m_ref.shape[1:])
            l_ref[n] = alpha * l_ref[n] + fold8(p, jnp.add)
            pv = _dot(vt[n * HEAD_DIM:(n + 1) * HEAD_DIM, :], p.astype(BF16))
            acc_ref[n] = alpha * acc_ref[n] + pv

    def run_tiles(tile, unroll):
        l_ref[...] = jnp.zeros(l_ref.shape, F32)
        acc_ref[...] = jnp.zeros(acc_ref.shape, F32)
        n_far = jnp.maximum(nt - 2, 0)

        def far_body(j, carry):
            for u in range(unroll):
                tile(j * unroll + u, None)
            return carry

        lax.fori_loop(0, n_far // unroll, far_body, 0)
        for u in range(unroll - 1):
            @pl.when(n_far % unroll > u)
            def _():
                tile((n_far // unroll) * unroll + u, None)

        @pl.when(nt >= 2)
        def _():
            tile(nt - 2, 0)
            tile(nt - 1, 1)

        @pl.when(nt < 2)
        def _():
            tile(nt - 1, 1)

    run_tiles(tile_fast, 2)
    l_min = jnp.min(jnp.minimum(jnp.sum(l_ref[0], axis=0, keepdims=True),
                                jnp.sum(l_ref[1], axis=0, keepdims=True)))

    @pl.when(jnp.logical_not(l_min >= L_UNDERFLOW))
    def _():
        m_ref[...] = jnp.full(m_ref.shape, neg_inf, F32)
        run_tiles(tile_exact, 1)

    o = [acc_ref[n] / jnp.sum(l_ref[n], axis=0, keepdims=True) for n in range(N_KV_HEADS)]
    for g in range(GROUP):
        pair = jnp.concatenate([o[n][:, g * tq:(g + 1) * tq] for n in range(N_KV_HEADS)], axis=0)
        out_ref[0, :, g * LANES:(g + 1) * LANES] = _dot_nt(eye_ref[...], pair.astype(BF16)).astype(BF16)


def _dsa_call(q, qi, wit, kid, kb, vb, bt, btmax, *, tq, q_off, n_valid_last, topk):
    bsz, sq, _ = q.shape
    nt_max, tk = kid.shape[1], kid.shape[2]
    idx_bits = int(nt_max * tk).bit_length()
    r4 = GROUP * tq
    tok = lambda n: pl.BlockSpec((1, tq, n), lambda b, i: (b, i, 0))
    keys = pl.BlockSpec((1, nt_max, tk, LANES), lambda b, i: (b, 0, 0, 0))
    kern = functools.partial(_dsa_kernel, tq=tq, tk=tk, q_off=q_off, n_valid_last=n_valid_last,
                             topk=topk, idx_bits=idx_bits)
    return pl.pallas_call(
        kern,
        grid=(bsz, sq // tq),
        in_specs=[tok(ATTN_DIM), tok(IDX_HEADS * IDX_DIM),
                  pl.BlockSpec((1, SUBLANES, tq), lambda b, i: (b, 0, i)), keys, keys,
                  pl.BlockSpec((1, nt_max, KV_DIM, tk), lambda b, i: (b, 0, 0, 0)),
                  pl.BlockSpec(bt.shape, lambda b, i: (0, 0, 0, 0)),
                  pl.BlockSpec(btmax.shape, lambda b, i: (0, 0, 0))],
        out_specs=tok(ATTN_DIM),
        out_shape=jax.ShapeDtypeStruct((bsz, sq, ATTN_DIM), BF16),
        scratch_shapes=[
            pltpu.VMEM((nt_max, tk, tq), I32),
            pltpu.VMEM((IDX_HEADS * tq, LANES), BF16),
            pltpu.VMEM((N_KV_HEADS, r4, LANES), BF16),
            pltpu.VMEM((tq, tq), BF16),
            pltpu.VMEM((SUBLANES, tq), I32),
            pltpu.VMEM((SUBLANES, LANES), F32),
            pltpu.VMEM((N_KV_HEADS, SUBLANES, r4), F32),
            pltpu.VMEM((N_KV_HEADS, SUBLANES, r4), F32),
            pltpu.VMEM((N_KV_HEADS, HEAD_DIM, r4), F32),
        ],
        compiler_params=pltpu.CompilerParams(
            dimension_semantics=("arbitrary", "arbitrary"), vmem_limit_bytes=VMEM_LIMIT),
        name="dsa",
    )(q, qi, wit, kid, kb, vb, bt, btmax)


def _t5_bucket(rel):
    nb = N_BUCKETS // 2
    max_exact = nb // 2
    ret = jnp.where(rel > 0, nb, 0)
    n = jnp.abs(rel)
    nf = jnp.maximum(n, 1).astype(F32)
    large = max_exact + (jnp.log(nf / max_exact) / math.log(MAX_DISTANCE / max_exact)
                         * (nb - max_exact)).astype(I32)
    large = jnp.minimum(large, nb - 1)
    return ret + jnp.where(n < max_exact, n, large)


def _bias_tables(rel_bias, tq, tk):
    far = rel_bias[_t5_bucket(jnp.full((1,), -(tk + 1), I32))[0]]
    period = 2 * tk + tq
    off = jnp.arange(period, dtype=I32)
    off = jnp.where(off < 2 * tk, off, off - period)
    diag = ((rel_bias[_t5_bucket(off - tk)] - far) * LOG2E).T
    tab = jnp.tile(diag, (1, tq))[:, :tq * (period - 1)].reshape(N_HEADS, tq, period - 1)[:, :, :2 * tk]
    tab = tab.reshape(N_KV_HEADS, GROUP, tq, 2, tk).transpose(3, 0, 4, 1, 2)
    tab = tab.reshape(2, N_KV_HEADS, tk, GROUP * tq).astype(F32)
    tab_max = jnp.maximum(jnp.max(tab, axis=(0, 2)), 0.0)
    return tab, jnp.broadcast_to(tab_max[:, None, :], (N_KV_HEADS, SUBLANES, GROUP * tq))


def _mix_kernel(attn_ref, sga_ref, pc_ref, x_ref, wao_ref, wout_ref, g_ref, wr_ref, br_ref,
                x1_ref, h2_ref, gates_ref):
    ya = _dot(attn_ref[...], wao_ref[...])
    merged = sga_ref[...] * ya + pc_ref[...]
    x1 = x_ref[...] + _dot(merged.astype(BF16), wout_ref[...])
    x1_ref[...] = x1
    hb = _rms(x1, g_ref[...]).astype(BF16)
    h2_ref[...] = hb

    rl = _dot(hb, wr_ref[...])
    tm = rl.shape[0]
    lane = lax.broadcasted_iota(I32, (tm, LANES), 1)
    neg_inf = F32(-jnp.inf)
    big = I32(LANES)

    def first_argmax(x):
        mx = jnp.max(x, axis=-1, keepdims=True)
        return jnp.min(jnp.where(x == mx, lane, big), axis=-1, keepdims=True)

    def pick(x, idx):
        return jnp.sum(jnp.where(lane == idx, x, 0.0), axis=-1, keepdims=True)

    glog = rl[:, :LANES]
    gvalid = lane < N_GROUPS
    g_sel = first_argmax(jnp.where(gvalid, glog + br_ref[0:1, :], neg_inf))
    gm = jnp.max(jnp.where(gvalid, glog, neg_inf), axis=-1, keepdims=True)
    gexp = jnp.where(gvalid, jnp.exp(glog - gm), 0.0)
    g_prob = pick(gexp, g_sel) / jnp.sum(gexp, axis=-1, keepdims=True)

    elog = rl[:, LANES:]
    in_grp = (lane // EXPERTS_PER_GROUP) == g_sel
    eb = jnp.where(in_grp, elog + br_ref[1:2, :], neg_inf)
    i1 = first_argmax(eb)
    i2 = first_argmax(jnp.where(lane == i1, neg_inf, eb))
    em = jnp.max(jnp.where(in_grp, elog, neg_inf), axis=-1, keepdims=True)
    eexp = jnp.where(in_grp, jnp.exp(elog - em), 0.0)
    esum = jnp.sum(eexp, axis=-1, keepdims=True)
    p1 = pick(eexp, i1) / esum
    p2 = pick(eexp, i2) / esum
    w1 = g_prob * p1 / (p1 + p2)
    w2 = g_prob * p2 / (p1 + p2)
    gates_ref[...] = jnp.where(lane == i1, w1, 0.0) + jnp.where(lane == i2, w2, 0.0)


def _mix_call(attn, sga, pc, x, wao, wout, g, wr, br, tm):
    n, d = x.shape
    tok = lambda c: pl.BlockSpec((tm, c), lambda i: (i, 0))
    full = lambda a: pl.BlockSpec(a.shape, lambda i: (0,) * a.ndim)
    return pl.pallas_call(
        _mix_kernel,
        grid=(n // tm,),
        in_specs=[tok(ATTN_DIM), tok(d), tok(d), tok(d), full(wao), full(wout), full(g), full(wr), full(br)],
        out_specs=(tok(d), tok(d), tok(LANES)),
        out_shape=(jax.ShapeDtypeStruct((n, d), F32), jax.ShapeDtypeStruct((n, d), BF16),
                   jax.ShapeDtypeStruct((n, LANES), F32)),
        compiler_params=pltpu.CompilerParams(
            dimension_semantics=("arbitrary",), vmem_limit_bytes=VMEM_LIMIT),
        name="mix",
    )(attn, sga, pc, x, wao, wout, g, wr, br)


def _moe_kernel(h_ref, gates_ref, x1_ref, wg_ref, wu_ref, wd_ref, g_ref, out_ref, acc_ref):
    e = pl.program_id(1)

    @pl.when(e == 0)
    def _():
        acc_ref[...] = jnp.zeros(acc_ref.shape, F32)

    hb = h_ref[...]
    a = _dot(hb, wg_ref[0])
    act = a * _sigmoid(a) * _dot(hb, wu_ref[0])
    y = _dot(act.astype(BF16), wd_ref[0])
    lane = lax.broadcasted_iota(I32, gates_ref.shape, 1)
    ge = jnp.sum(jnp.where(lane == e, gates_ref[...], 0.0), axis=-1, keepdims=True)
    acc_ref[...] = acc_ref[...] + ge * y

    @pl.when(e == pl.num_programs(1) - 1)
    def _():
        out_ref[...] = _rms(x1_ref[...] + acc_ref[...], g_ref[...])


def _moe_call(hb, gates, x1, wg, wu, wd, g, tm):
    n, d = x1.shape
    ne, _, de = wg.shape
    tok = lambda c: pl.BlockSpec((tm, c), lambda i, e: (i, 0))
    return pl.pallas_call(
        _moe_kernel,
        grid=(n // tm, ne),
        in_specs=[tok(d), tok(LANES), tok(d),
                  pl.BlockSpec((1, d, de), lambda i, e: (e, 0, 0)),
                  pl.BlockSpec((1, d, de), lambda i, e: (e, 0, 0)),
                  pl.BlockSpec((1, de, d), lambda i, e: (e, 0, 0)),
                  pl.BlockSpec(g.shape, lambda i, e: (0, 0))],
        out_specs=tok(d),
        out_shape=jax.ShapeDtypeStruct((n, d), F32),
        scratch_shapes=[pltpu.VMEM((tm, d), F32)],
        compiler_params=pltpu.CompilerParams(
            dimension_semantics=("arbitrary", "arbitrary"), vmem_limit_bytes=VMEM_LIMIT),
        name="moe",
    )(hb, gates, x1, wg, wu, wd, g)


def _head_pair_perm():
    cols = []
    for g in range(GROUP):
        for n in range(N_KV_HEADS):
            h = n * GROUP + g
            cols.extend(range(h * HEAD_DIM, (h + 1) * HEAD_DIM))
    return np.asarray(cols, np.int32)


def _prep_weights(w_in, w_attn_out, w_conv_out, w_out, w_group, b_group, w_router, b_router,
                  w_gate, w_up, w_down):
    d = w_in.shape[0]
    o = np.cumsum([0, ATTN_DIM, KV_DIM, KV_DIM, IDX_HEADS * IDX_DIM, IDX_DIM, IDX_HEADS,
                   CONV_DIM, CONV_DIM, CONV_DIM, d, d])
    perm = _head_pair_perm()
    wq = w_in[:, o[0]:o[1]][:, perm] * (ATTN_SCALE * LOG2E)
    wa = jnp.concatenate([wq, w_in[:, o[1]:o[3]]], axis=1).astype(BF16)
    wki = w_in[:, o[4]:o[5]]
    wwi = jnp.pad(w_in[:, o[5]:o[6]] * IDX_W_SCALE, ((0, 0), (0, LANES - IDX_HEADS)))
    wb = jnp.concatenate([w_in[:, o[3]:o[4]] * IDX_SCALE, wki, wki, wwi], axis=1).astype(BF16)
    wc = w_in[:, o[6]:o[9]].astype(BF16)
    wg = w_in[:, o[9]:o[11]].astype(BF16)
    wr = jnp.concatenate([jnp.pad(w_group, ((0, 0), (0, LANES - N_GROUPS))),
                          jnp.pad(w_router, ((0, 0), (0, LANES - N_EXPERTS)))], axis=1).astype(BF16)
    br = jnp.stack([jnp.pad(b_group, (0, LANES - N_GROUPS)),
                    jnp.pad(b_router, (0, LANES - N_EXPERTS))]).astype(F32)
    return dict(wa=wa, wb=wb, wc=wc, wg=wg, wao=w_attn_out[perm, :].astype(BF16),
                wco=w_conv_out.astype(BF16), wout=w_out.astype(BF16), wr=wr, br=br,
                wgate=w_gate.astype(BF16), wup=w_up.astype(BF16), wdown=w_down.astype(BF16))


def _tile(n, pref):
    t = min(n, pref)
    assert n % t == 0, (n, pref)
    return t


def _layer(x, past, w, norm_mix, w_conv, norm_ffn, rel_bias):
    bsz, t, d = x.shape
    tk = KEY_TILE
    cinit = jnp.zeros((bsz, CONV_W - 1, CONV_DIM), F32) if past is None else past[3].astype(F32)
    (q, k, v, kb, vb, qi, kidx, kid, wi, sga, pc, cst) = _proj_call(
        x, cinit, norm_mix[None, :], w["wa"], w["wb"], w["wc"], w["wg"], w_conv, w["wco"], _tile(t, 256))

    if past is None:
        assert t % tk == 0
        tq, q_off, total = tk, 0, t
        n_valid_last = tk
        kid_all, kb_all, vb_all = kid, kb, vb
    else:
        p_len = past[0].shape[1]
        tq = ROW_STRIP
        assert p_len % tk == 0 and t <= tq
        q_off, total = p_len, p_len + t
        n_valid_last = t
        qpad = ((0, 0), (0, tq - t), (0, 0))
        q, qi, wi = jnp.pad(q, qpad), jnp.pad(qi, qpad), jnp.pad(wi, qpad)
        pad = ((0, 0), (0, tk - t), (0, 0))
        pk = past[0].reshape(bsz, p_len, KV_DIM).astype(BF16)
        pv = past[1].reshape(bsz, p_len, KV_DIM).astype(BF16)
        pki = past[2].astype(BF16)
        kb_all = jnp.concatenate([pk, jnp.pad(kb, pad)], axis=1)
        vb_all = jnp.concatenate([pv, jnp.pad(vb, pad)], axis=1)
        kid_all = jnp.concatenate([jnp.concatenate([pki, pki], axis=-1), jnp.pad(kid, pad)], axis=1)
    nt = kb_all.shape[1] // tk
    tiles = lambda a: a.reshape(bsz, nt, tk, LANES)
    wit = jnp.swapaxes(wi[:, :, :SUBLANES], 1, 2)
    vbt = jnp.swapaxes(tiles(vb_all), 2, 3)
    bt, btmax = _bias_tables(rel_bias, tq, tk)
    attn = _dsa_call(q, qi, wit, tiles(kid_all), tiles(kb_all), vbt, bt, btmax, tq=tq, q_off=q_off,
                     n_valid_last=n_valid_last, topk=min(TOPK_MAX, total // 4))
    attn = attn[:, :t]

    n = bsz * t
    tm = _tile(n, 512)
    x1, h2, gates = _mix_call(attn.reshape(n, ATTN_DIM), sga.reshape(n, d), pc.reshape(n, d),
                              x.reshape(n, d), w["wao"], w["wout"], norm_ffn[None, :], w["wr"], w["br"], tm)
    state = (k.reshape(bsz, t, N_KV_HEADS, HEAD_DIM), v.reshape(bsz, t, N_KV_HEADS, HEAD_DIM), kidx, cst)
    return x1, h2, gates, state


def kernel(x_prompt, x_sample, cache_k, cache_v, cache_kidx, state_conv, rel_bias, norm_mix, w_in,
           w_attn_out, w_conv, w_conv_out, w_out, norm_ffn, w_group, b_group, w_router, b_router,
           w_gate, w_up, w_down, norm_final):
    depth = w_in.shape[0]
    assert depth == 1, "the final norm is fused into the last layer's expert kernel"
    l = 0
    w = _prep_weights(w_in[l], w_attn_out[l], w_conv_out[l], w_out[l], w_group[l], b_group[l],
                      w_router[l], b_router[l], w_gate[l], w_up[l], w_down[l])
    outs = []
    states = []
    for x, past in ((x_prompt, None),
                    (x_sample, (cache_k[l], cache_v[l], cache_kidx[l], state_conv[l]))):
        bsz, t, d = x.shape
        x1, h2, gates, state = _layer(x, past, w, norm_mix[l], w_conv[l], norm_ffn[l], rel_bias)
        n = bsz * t
        y = _moe_call(h2, gates, x1, w["wgate"], w["wup"], w["wdown"], norm_final[None, :], _tile(n, 512))
        outs.append(y.reshape(bsz, t, d))
        states.append(state)
    sp, ss = states
    return (outs[0], outs[1], sp[0][None], sp[1][None], sp[2][None], sp[3][None],
            ss[0][None], ss[1][None], ss[2][None], ss[3][None])
```

```python
import functools
import math

import jax
import jax.numpy as jnp
import numpy as np
from jax import lax
from jax.experimental import pallas as pl
from jax.experimental.pallas import tpu as pltpu

F32 = jnp.float32
BF16 = jnp.bfloat16
I32 = jnp.int32

CHUNK = 64
N_HEADS = 8
N_KV_HEADS = 2
GROUP = N_HEADS // N_KV_HEADS
HEAD_DIM = 64
ATTN_DIM = N_HEADS * HEAD_DIM
KV_DIM = N_KV_HEADS * HEAD_DIM
IDX_HEADS = 4
IDX_DIM = 64
TOPK_MAX = 256
CONV_DIM = 512
CONV_W = 3
N_BUCKETS = 32
MAX_DISTANCE = 128
N_GROUPS = 4
EXPERTS_PER_GROUP = 4
N_EXPERTS = N_GROUPS * EXPERTS_PER_GROUP
EPS = 1e-6
ATTN_SCALE = HEAD_DIM ** -0.5
IDX_SCALE = IDX_DIM ** -0.5
IDX_W_SCALE = IDX_HEADS ** -0.5

LANES = 128
SUBLANES = 8
KEY_TILE = 256
ROW_STRIP = 128
VMEM_LIMIT = 56 * 1024 * 1024
INT_MIN = -2 ** 31
IDX_BIG = 2 ** 30
F32_LOWEST = float(np.finfo(np.float32).min)
LOG2E = math.log2(math.e)
BOUND_SLACK = 1.0 + 2.0 ** -5
L_UNDERFLOW = 2.0 ** -80


def _dot(a, b):
    return jnp.dot(a, b, preferred_element_type=F32)


def _dot_nt(a, b):
    return lax.dot_general(a, b, (((1,), (1,)), ((), ())), preferred_element_type=F32)


def _sigmoid(x):
    return 1.0 / (1.0 + jnp.exp(-x))


def _rms(x, g):
    return x * lax.rsqrt(jnp.mean(x * x, axis=-1, keepdims=True) + EPS) * g


def _proj_kernel(x_ref, cinit_ref, g_ref, wa_ref, wb_ref, wc_ref, wg_ref, wconv_ref, wco_ref,
                 q_ref, k_ref, v_ref, kb_ref, vb_ref, qi_ref, kidx_ref, kid_ref, wi_ref,
                 sga_ref, pc_ref, cst_ref, carry_ref, *, tm):
    j = pl.program_id(1)
    hb = _rms(x_ref[0], g_ref[...]).astype(BF16)

    a = _dot(hb, wa_ref[...])
    q_ref[0] = a[:, :ATTN_DIM].astype(BF16)
    k = a[:, ATTN_DIM:ATTN_DIM + KV_DIM]
    v = a[:, ATTN_DIM + KV_DIM:]
    k_ref[0] = k
    v_ref[0] = v
    kb_ref[0] = k.astype(BF16)
    vb_ref[0] = v.astype(BF16)

    b = _dot(hb, wb_ref[...])
    nqi = IDX_HEADS * IDX_DIM
    qi_ref[0] = b[:, :nqi].astype(BF16)
    kid_ref[0] = b[:, nqi:nqi + LANES].astype(BF16)
    kidx_ref[0] = b[:, nqi:nqi + IDX_DIM]
    wi_ref[0] = b[:, nqi + LANES:]

    c = _dot(hb, wc_ref[...])
    cb = c[:, :CONV_DIM]
    u = c[:, CONV_DIM:2 * CONV_DIM] * c[:, 2 * CONV_DIM:]

    @pl.when(j == 0)
    def _():
        carry_ref[6:8, :] = cinit_ref[0]

    c2 = carry_ref[6:7, :]
    c1 = carry_ref[7:8, :]
    row = lax.broadcasted_iota(I32, (tm, CONV_DIM), 0)
    up1 = jnp.where(row == 0, c1, pltpu.roll(u, 1, axis=0))
    up2 = jnp.where(row == 0, c2, jnp.where(row == 1, c1, pltpu.roll(u, 2, axis=0)))
    conv = wconv_ref[0:1, :] * up2 + wconv_ref[1:2, :] * up1 + wconv_ref[2:3, :] * u
    tail = u[tm - 2:tm, :]
    carry_ref[6:8, :] = tail
    cst_ref[0] = tail
    yc = _dot((cb * conv).astype(BF16), wco_ref[...])

    gt = _dot(hb, wg_ref[...])
    d = yc.shape[-1]
    sga_ref[0] = _sigmoid(gt[:, :d])
    pc_ref[0] = _sigmoid(gt[:, d:]) * yc


def _proj_call(x, cinit, g, wa, wb, wc, wg, wconv, wco, tm):
    bsz, s, d = x.shape
    grid = (bsz, s // tm)
    tok = lambda n: pl.BlockSpec((1, tm, n), lambda b, j: (b, j, 0))
    full = lambda a: pl.BlockSpec(a.shape, lambda b, j: (0,) * a.ndim)
    st = pl.BlockSpec((1, CONV_W - 1, CONV_DIM), lambda b, j: (b, 0, 0))
    out_shape = (
        jax.ShapeDtypeStruct((bsz, s, ATTN_DIM), BF16),
        jax.ShapeDtypeStruct((bsz, s, KV_DIM), F32),
        jax.ShapeDtypeStruct((bsz, s, KV_DIM), F32),
        jax.ShapeDtypeStruct((bsz, s, KV_DIM), BF16),
        jax.ShapeDtypeStruct((bsz, s, KV_DIM), BF16),
        jax.ShapeDtypeStruct((bsz, s, IDX_HEADS * IDX_DIM), BF16),
        jax.ShapeDtypeStruct((bsz, s, IDX_DIM), F32),
        jax.ShapeDtypeStruct((bsz, s, LANES), BF16),
        jax.ShapeDtypeStruct((bsz, s, LANES), F32),
        jax.ShapeDtypeStruct((bsz, s, d), F32),
        jax.ShapeDtypeStruct((bsz, s, d), F32),
        jax.ShapeDtypeStruct((bsz, CONV_W - 1, CONV_DIM), F32),
    )
    out_specs = (tok(ATTN_DIM), tok(KV_DIM), tok(KV_DIM), tok(KV_DIM), tok(KV_DIM),
                 tok(IDX_HEADS * IDX_DIM), tok(IDX_DIM), tok(LANES), tok(LANES), tok(d), tok(d), st)
    return pl.pallas_call(
        functools.partial(_proj_kernel, tm=tm),
        grid=grid,
        in_specs=[tok(d), st, full(g), full(wa), full(wb), full(wc), full(wg), full(wconv), full(wco)],
        out_specs=out_specs,
        out_shape=out_shape,
        scratch_shapes=[pltpu.VMEM((8, CONV_DIM), F32)],
        compiler_params=pltpu.CompilerParams(
            dimension_semantics=("arbitrary", "arbitrary"), vmem_limit_bytes=VMEM_LIMIT),
        name="proj",
    )(x, cinit, g, wa, wb, wc, wg, wconv, wco)


def _tree_sum(xs):
    while len(xs) > 1:
        xs = [xs[a] + xs[a + 1] for a in range(0, len(xs) - 1, 2)] + ([xs[-1]] if len(xs) % 2 else [])
    return xs[0]


def _dsa_kernel(q_ref, qi_ref, wit_ref, kid_ref, kb_ref, vbt_ref, bt_ref, btmax_ref, out_ref,
                keys_ref, qis_ref, qs_ref, eye_ref, jcut_ref, kmax_ref, m_ref, l_ref, acc_ref,
                *, tq, tk, q_off, n_valid_last, topk, idx_bits):
    i = pl.program_id(1)
    nt = (q_off + i * tq) // tk + 1
    neg_inf = F32(-jnp.inf)

    lane = lax.broadcasted_iota(I32, (tq, LANES), 1)
    lo = lane < HEAD_DIM
    zero_b = jnp.zeros((tq, LANES), BF16)
    qi = qi_ref[0]
    for h in range(IDX_HEADS):
        blk = qi[:, (h // 2) * LANES:(h // 2 + 1) * LANES]
        qis_ref[h * tq:(h + 1) * tq, :] = jnp.where(lo if h % 2 == 0 else ~lo, blk, zero_b)
    q = q_ref[0]
    for n in range(N_KV_HEADS):
        for g in range(GROUP):
            blk = q[:, g * LANES:(g + 1) * LANES]
            qs_ref[n, g * tq:(g + 1) * tq, :] = jnp.where(lo if n == 0 else ~lo, blk, zero_b)
    eye_ref[...] = jnp.where(lax.broadcasted_iota(I32, (tq, tq), 0) == lax.broadcasted_iota(I32, (tq, tq), 1),
                             1.0, 0.0).astype(BF16)
    w_row = jnp.concatenate([wit_ref[0, h:h + 1, :] for h in range(IDX_HEADS)], axis=1)

    krow = lax.broadcasted_iota(I32, (tk, tq), 0)
    qcol = lax.broadcasted_iota(I32, (tk, tq), 1)
    lim_last = jnp.minimum(((qcol >> 6) + 1) << 6, n_valid_last)

    def score_tiles(ts):
        ss = [_dot_nt(kid_ref[0, t], qis_ref[...]) for t in ts]
        for t, s in zip(ts, ss):
            s = jnp.maximum(s, 0.0) * w_row
            sc = (s[:, 0:tq] + s[:, tq:2 * tq]) + (s[:, 2 * tq:3 * tq] + s[:, 3 * tq:4 * tq])
            sc = jnp.where(sc == 0.0, 0.0, sc)
            bits = lax.bitcast_convert_type(sc, I32)
            key = bits ^ ((bits >> 31) & 0x7FFFFFFF)
            lim = jnp.where(t == nt - 1, lim_last, tk)
            keys_ref[t] = jnp.where(krow < lim, key, INT_MIN)

    def score_body(j, carry):
        score_tiles([2 * j, 2 * j + 1])
        return carry

    lax.fori_loop(0, nt // 2, score_body, 0)

    @pl.when(nt % 2 == 1)
    def _():
        score_tiles([nt - 1])

    def count(ind_fn, cand):
        cand8 = jnp.broadcast_to(cand, (SUBLANES, tq))
        row8 = lax.broadcasted_iota(I32, (SUBLANES, tq), 0)

        def body(t, acc):
            parts = []
            for r in range(tk // SUBLANES):
                x = keys_ref[t, r * SUBLANES:(r + 1) * SUBLANES, :]
                parts.append(ind_fn(x, row8 + (t * tk + r * SUBLANES), cand8))
            return acc + _tree_sum(parts)

        acc = lax.fori_loop(0, nt, body, jnp.zeros((SUBLANES, tq), I32))
        return jnp.sum(acc.astype(F32), axis=0, keepdims=True)

    ge = lambda x, idx, c: jnp.where(x >= c, 1, 0)

    def bit_body(b, thr):
        cand = thr + (jnp.int32(1) << (31 - b))
        return jnp.where(count(ge, cand) >= topk, cand, thr)

    thr = lax.fori_loop(0, 32, bit_body, jnp.full((1, tq), INT_MIN, I32))
    thr8 = jnp.broadcast_to(thr, (SUBLANES, tq))

    jcut_ref[...] = jnp.full(jcut_ref.shape, IDX_BIG, I32)
    n_ge = count(ge, thr)

    @pl.when(jnp.max(n_ge) > topk)
    def _():
        need = topk - count(ge, thr + 1)
        tied_lt = lambda x, idx, c: jnp.where(x == thr8, jnp.where(idx < c, 1, 0), 0)

        def jbit_body(b, jj):
            cand = jj + (jnp.int32(1) << (idx_bits - 1 - b))
            return jnp.where(count(tied_lt, cand) <= need, cand, jj)

        jj = lax.fori_loop(0, idx_bits, jbit_body, jnp.zeros((1, tq), I32))
        jcut_ref[...] = jnp.broadcast_to(jj, jcut_ref.shape)

    jcut = jnp.where(thr == INT_MIN, 0, jcut_ref[0:1, :])

    @pl.when(i == 0)
    def _():
        half0 = lax.broadcasted_iota(I32, (tk, LANES), 1) < HEAD_DIM

        def kbody(t, c):
            k = kb_ref[0, t].astype(F32)
            sq = k * k
            s0 = jnp.max(jnp.sum(jnp.where(half0, sq, 0.0), axis=1, keepdims=True), axis=0, keepdims=True)
            s1 = jnp.max(jnp.sum(jnp.where(half0, 0.0, sq), axis=1, keepdims=True), axis=0, keepdims=True)
            return jnp.maximum(c[0], s0), jnp.maximum(c[1], s1)

        z = jnp.zeros((1, 1), F32)
        c = lax.fori_loop(0, kb_ref.shape[1], kbody, (z, z))
        for n in range(N_KV_HEADS):
            kmax_ref[n:n + 1, :] = jnp.broadcast_to(jnp.sqrt(c[n]), (1, LANES))

    ones8 = jnp.ones((SUBLANES, LANES), BF16)
    mb = []
    for n in range(N_KV_HEADS):
        qf = qs_ref[n].astype(F32)
        qsq = _dot_nt(ones8, (qf * qf).astype(BF16))[0:1, :]
        mb.append(BOUND_SLACK * jnp.sqrt(qsq) * kmax_ref[n:n + 1, 0:1] + btmax_ref[n, 0:1, :])

    def fold8(x, op):
        parts = [x[r * SUBLANES:(r + 1) * SUBLANES, :] for r in range(x.shape[0] // SUBLANES)]
        while len(parts) > 1:
            parts = [op(parts[a], parts[a + 1]) for a in range(0, len(parts), 2)]
        return parts[0]

    def logits(t, near, n, sel4):
        lg = _dot_nt(kb_ref[0, t], qs_ref[n]) + sel4
        if near is not None:
            lg = lg + bt_ref[near, n]
        return lg

    def selection(t):
        key = keys_ref[t]
        sel = jnp.where(key > thr, 0.0,
                        jnp.where(key == thr, jnp.where(krow + t * tk < jcut, 0.0, neg_inf), neg_inf))
        return jnp.concatenate([sel] * GROUP, axis=1)

    def tiles_fast(specs):
        sel4 = [selection(t) for t, _ in specs]
        units = [(t, near, n) for t, near in specs for n in range(N_KV_HEADS)]
        lgs = [logits(t, near, n, sel4[u // N_KV_HEADS]) for u, (t, near, n) in enumerate(units)]
        ps = [jnp.exp2(lg - mb[n]) for lg, (_, _, n) in zip(lgs, units)]
        for n in range(N_KV_HEADS):
            l_ref[n] = l_ref[n] + _tree_sum([fold8(p, jnp.add) for p, u in zip(ps, units) if u[2] == n])
        pvs = [_dot(vbt_ref[0, t, n * HEAD_DIM:(n + 1) * HEAD_DIM, :], p.astype(BF16))
               for p, (t, _, n) in zip(ps, units)]
        for n in range(N_KV_HEADS):
            acc_ref[n] = acc_ref[n] + _tree_sum([pv for pv, u in zip(pvs, units) if u[2] == n])

    def tiles_exact(specs):
        for t, near in specs:
            tile_exact(t, near)

    def tile_exact(t, near):
        sel4 = selection(t)
        vt = vbt_ref[0, t]
        for n in range(N_KV_HEADS):
            lg = logits(t, near, n, sel4)
            m_old = m_ref[n, 0:1, :]
            m_new = jnp.maximum(m_old, jnp.max(fold8(lg, jnp.maximum), axis=0, keepdims=True))
            m_safe = jnp.maximum(m_new, F32_LOWEST)
            alpha = jnp.exp2(m_old - m_safe)
            p = jnp.exp2(lg - m_safe)
            m_ref[n] = jnp.broadcast_to(m_new, m_ref.shape[1:])
            l_ref[n] = alpha * l_ref[n] + fold8(p, jnp.add)
            pv = _dot(vt[n * HEAD_DIM:(n + 1) * HEAD_DIM, :], p.astype(BF16))
            acc_ref[n] = alpha * acc_ref[n] + pv

    def run_tiles(tiles, unroll):
        l_ref[...] = jnp.zeros(l_ref.shape, F32)
        acc_ref[...] = jnp.zeros(acc_ref.shape, F32)
        n_far = jnp.maximum(nt - 2, 0)

        def far_body(j, carry):
            tiles([(j * unroll + u, None) for u in range(unroll)])
            return carry

        lax.fori_loop(0, n_far // unroll, far_body, 0)
        for u in range(unroll - 1):
            @pl.when(n_far % unroll > u)
            def _():
                tiles([((n_far // unroll) * unroll + u, None)])

        @pl.when(nt >= 2)
        def _():
            tiles([(nt - 2, 0), (nt - 1, 1)])

        @pl.when(nt < 2)
        def _():
            tiles([(nt - 1, 1)])

    run_tiles(tiles_fast, 2)
    l_min = jnp.min(jnp.minimum(jnp.sum(l_ref[0], axis=0, keepdims=True),
                                jnp.sum(l_ref[1], axis=0, keepdims=True)))

    @pl.when(jnp.logical_not(l_min >= L_UNDERFLOW))
    def _():
        m_ref[...] = jnp.full(m_ref.shape, neg_inf, F32)
        run_tiles(tiles_exact, 1)

    o = [acc_ref[n] / jnp.sum(l_ref[n], axis=0, keepdims=True) for n in range(N_KV_HEADS)]
    for g in range(GROUP):
        pair = jnp.concatenate([o[n][:, g * tq:(g + 1) * tq] for n in range(N_KV_HEADS)], axis=0)
        out_ref[0, :, g * LANES:(g + 1) * LANES] = _dot_nt(eye_ref[...], pair.astype(BF16)).astype(BF16)


def _dsa_call(q, qi, wit, kid, kb, vb, bt, btmax, *, tq, q_off, n_valid_last, topk):
    bsz, sq, _ = q.shape
    nt_max, tk = kid.shape[1], kid.shape[2]
    idx_bits = int(nt_max * tk).bit_length()
    r4 = GROUP * tq
    tok = lambda n: pl.BlockSpec((1, tq, n), lambda b, i: (b, i, 0))
    keys = pl.BlockSpec((1, nt_max, tk, LANES), lambda b, i: (b, 0, 0, 0))
    kern = functools.partial(_dsa_kernel, tq=tq, tk=tk, q_off=q_off, n_valid_last=n_valid_last,
                             topk=topk, idx_bits=idx_bits)
    return pl.pallas_call(
        kern,
        grid=(bsz, sq // tq),
        in_specs=[tok(ATTN_DIM), tok(IDX_HEADS * IDX_DIM),
                  pl.BlockSpec((1, SUBLANES, tq), lambda b, i: (b, 0, i)), keys, keys,
                  pl.BlockSpec((1, nt_max, KV_DIM, tk), lambda b, i: (b, 0, 0, 0)),
                  pl.BlockSpec(bt.shape, lambda b, i: (0, 0, 0, 0)),
                  pl.BlockSpec(btmax.shape, lambda b, i: (0, 0, 0))],
        out_specs=tok(ATTN_DIM),
        out_shape=jax.ShapeDtypeStruct((bsz, sq, ATTN_DIM), BF16),
        scratch_shapes=[
            pltpu.VMEM((nt_max, tk, tq), I32),
            pltpu.VMEM((IDX_HEADS * tq, LANES), BF16),
            pltpu.VMEM((N_KV_HEADS, r4, LANES), BF16),
            pltpu.VMEM((tq, tq), BF16),
            pltpu.VMEM((SUBLANES, tq), I32),
            pltpu.VMEM((SUBLANES, LANES), F32),
            pltpu.VMEM((N_KV_HEADS, SUBLANES, r4), F32),
            pltpu.VMEM((N_KV_HEADS, SUBLANES, r4), F32),
            pltpu.VMEM((N_KV_HEADS, HEAD_DIM, r4), F32),
        ],
        compiler_params=pltpu.CompilerParams(
            dimension_semantics=("arbitrary", "arbitrary"), vmem_limit_bytes=VMEM_LIMIT),
        name="dsa",
    )(q, qi, wit, kid, kb, vb, bt, btmax)


def _t5_bucket(rel):
    nb = N_BUCKETS // 2
    max_exact = nb // 2
    ret = jnp.where(rel > 0, nb, 0)
    n = jnp.abs(rel)
    nf = jnp.maximum(n, 1).astype(F32)
    large = max_exact + (jnp.log(nf / max_exact) / math.log(MAX_DISTANCE / max_exact)
                         * (nb - max_exact)).astype(I32)
    large = jnp.minimum(large, nb - 1)
    return ret + jnp.where(n < max_exact, n, large)


def _bias_tables(rel_bias, tq, tk):
    far = rel_bias[_t5_bucket(jnp.full((1,), -(tk + 1), I32))[0]]
    period = 2 * tk + tq
    off = jnp.arange(period, dtype=I32)
    off = jnp.where(off < 2 * tk, off, off - period)
    diag = ((rel_bias[_t5_bucket(off - tk)] - far) * LOG2E).T
    tab = jnp.tile(diag, (1, tq))[:, :tq * (period - 1)].reshape(N_HEADS, tq, period - 1)[:, :, :2 * tk]
    tab = tab.reshape(N_KV_HEADS, GROUP, tq, 2, tk).transpose(3, 0, 4, 1, 2)
    tab = tab.reshape(2, N_KV_HEADS, tk, GROUP * tq).astype(F32)
    tab_max = jnp.maximum(jnp.max(tab, axis=(0, 2)), 0.0)
    return tab, jnp.broadcast_to(tab_max[:, None, :], (N_KV_HEADS, SUBLANES, GROUP * tq))


def _mix_kernel(attn_ref, sga_ref, pc_ref, x_ref, wao_ref, wout_ref, g_ref, wr_ref, br_ref,
                x1_ref, h2_ref, gates_ref):
    ya = _dot(attn_ref[...], wao_ref[...])
    merged = sga_ref[...] * ya + pc_ref[...]
    x1 = x_ref[...] + _dot(merged.astype(BF16), wout_ref[...])
    x1_ref[...] = x1
    hb = _rms(x1, g_ref[...]).astype(BF16)
    h2_ref[...] = hb

    rl = _dot(hb, wr_ref[...])
    tm = rl.shape[0]
    lane = lax.broadcasted_iota(I32, (tm, LANES), 1)
    neg_inf = F32(-jnp.inf)
    big = I32(LANES)

    def first_argmax(x):
        mx = jnp.max(x, axis=-1, keepdims=True)
        return jnp.min(jnp.where(x == mx, lane, big), axis=-1, keepdims=True)

    def pick(x, idx):
        return jnp.sum(jnp.where(lane == idx, x, 0.0), axis=-1, keepdims=True)

    glog = rl[:, :LANES]
    gvalid = lane < N_GROUPS
    g_sel = first_argmax(jnp.where(gvalid, glog + br_ref[0:1, :], neg_inf))
    gm = jnp.max(jnp.where(gvalid, glog, neg_inf), axis=-1, keepdims=True)
    gexp = jnp.where(gvalid, jnp.exp(glog - gm), 0.0)
    g_prob = pick(gexp, g_sel) / jnp.sum(gexp, axis=-1, keepdims=True)

    elog = rl[:, LANES:]
    in_grp = (lane // EXPERTS_PER_GROUP) == g_sel
    eb = jnp.where(in_grp, elog + br_ref[1:2, :], neg_inf)
    i1 = first_argmax(eb)
    i2 = first_argmax(jnp.where(lane == i1, neg_inf, eb))
    em = jnp.max(jnp.where(in_grp, elog, neg_inf), axis=-1, keepdims=True)
    eexp = jnp.where(in_grp, jnp.exp(elog - em), 0.0)
    esum = jnp.sum(eexp, axis=-1, keepdims=True)
    p1 = pick(eexp, i1) / esum
    p2 = pick(eexp, i2) / esum
    w1 = g_prob * p1 / (p1 + p2)
    w2 = g_prob * p2 / (p1 + p2)
    gates_ref[...] = jnp.where(lane == i1, w1, 0.0) + jnp.where(lane == i2, w2, 0.0)


def _mix_call(attn, sga, pc, x, wao, wout, g, wr, br, tm):
    n, d = x.shape
    tok = lambda c: pl.BlockSpec((tm, c), lambda i: (i, 0))
    full = lambda a: pl.BlockSpec(a.shape, lambda i: (0,) * a.ndim)
    return pl.pallas_call(
        _mix_kernel,
        grid=(n // tm,),
        in_specs=[tok(ATTN_DIM), tok(d), tok(d), tok(d), full(wao), full(wout), full(g), full(wr), full(br)],
        out_specs=(tok(d), tok(d), tok(LANES)),
        out_shape=(jax.ShapeDtypeStruct((n, d), F32), jax.ShapeDtypeStruct((n, d), BF16),
                   jax.ShapeDtypeStruct((n, LANES), F32)),
        compiler_params=pltpu.CompilerParams(
            dimension_semantics=("arbitrary",), vmem_limit_bytes=VMEM_LIMIT),
        name="mix",
    )(attn, sga, pc, x, wao, wout, g, wr, br)


def _moe_kernel(h_ref, gates_ref, x1_ref, wg_ref, wu_ref, wd_ref, g_ref, out_ref, acc_ref):
    e = pl.program_id(1)

    @pl.when(e == 0)
    def _():
        acc_ref[...] = jnp.zeros(acc_ref.shape, F32)

    hb = h_ref[...]
    a = _dot(hb, wg_ref[0])
    act = a * _sigmoid(a) * _dot(hb, wu_ref[0])
    y = _dot(act.astype(BF16), wd_ref[0])
    lane = lax.broadcasted_iota(I32, gates_ref.shape, 1)
    ge = jnp.sum(jnp.where(lane == e, gates_ref[...], 0.0), axis=-1, keepdims=True)
    acc_ref[...] = acc_ref[...] + ge * y

    @pl.when(e == pl.num_programs(1) - 1)
    def _():
        out_ref[...] = _rms(x1_ref[...] + acc_ref[...], g_ref[...])


def _moe_call(hb, gates, x1, wg, wu, wd, g, tm):
    n, d = x1.shape
    ne, _, de = wg.shape
    tok = lambda c: pl.BlockSpec((tm, c), lambda i, e: (i, 0))
    return pl.pallas_call(
        _moe_kernel,
        grid=(n // tm, ne),
        in_specs=[tok(d), tok(LANES), tok(d),
                  pl.BlockSpec((1, d, de), lambda i, e: (e, 0, 0)),
                  pl.BlockSpec((1, d, de), lambda i, e: (e, 0, 0)),
                  pl.BlockSpec((1, de, d), lambda i, e: (e, 0, 0)),
                  pl.BlockSpec(g.shape, lambda i, e: (0, 0))],
        out_specs=tok(d),
        out_shape=jax.ShapeDtypeStruct((n, d), F32),
        scratch_shapes=[pltpu.VMEM((tm, d), F32)],
        compiler_params=pltpu.CompilerParams(
            dimension_semantics=("arbitrary", "arbitrary"), vmem_limit_bytes=VMEM_LIMIT),
        name="moe",
    )(hb, gates, x1, wg, wu, wd, g)


def _head_pair_perm():
    cols = []
    for g in range(GROUP):
        for n in range(N_KV_HEADS):
            h = n * GROUP + g
            cols.extend(range(h * HEAD_DIM, (h + 1) * HEAD_DIM))
    return np.asarray(cols, np.int32)


def _prep_weights(w_in, w_attn_out, w_conv_out, w_out, w_group, b_group, w_router, b_router,
                  w_gate, w_up, w_down):
    d = w_in.shape[0]
    o = np.cumsum([0, ATTN_DIM, KV_DIM, KV_DIM, IDX_HEADS * IDX_DIM, IDX_DIM, IDX_HEADS,
                   CONV_DIM, CONV_DIM, CONV_DIM, d, d])
    perm = _head_pair_perm()
    wq = w_in[:, o[0]:o[1]][:, perm] * (ATTN_SCALE * LOG2E)
    wa = jnp.concatenate([wq, w_in[:, o[1]:o[3]]], axis=1).astype(BF16)
    wki = w_in[:, o[4]:o[5]]
    wwi = jnp.pad(w_in[:, o[5]:o[6]] * IDX_W_SCALE, ((0, 0), (0, LANES - IDX_HEADS)))
    wb = jnp.concatenate([w_in[:, o[3]:o[4]] * IDX_SCALE, wki, wki, wwi], axis=1).astype(BF16)
    wc = w_in[:, o[6]:o[9]].astype(BF16)
    wg = w_in[:, o[9]:o[11]].astype(BF16)
    wr = jnp.concatenate([jnp.pad(w_group, ((0, 0), (0, LANES - N_GROUPS))),
                          jnp.pad(w_router, ((0, 0), (0, LANES - N_EXPERTS)))], axis=1).astype(BF16)
    br = jnp.stack([jnp.pad(b_group, (0, LANES - N_GROUPS)),
                    jnp.pad(b_router, (0, LANES - N_EXPERTS))]).astype(F32)
    return dict(wa=wa, wb=wb, wc=wc, wg=wg, wao=w_attn_out[perm, :].astype(BF16),
                wco=w_conv_out.astype(BF16), wout=w_out.astype(BF16), wr=wr, br=br,
                wgate=w_gate.astype(BF16), wup=w_up.astype(BF16), wdown=w_down.astype(BF16))


def _tile(n, pref):
    t = min(n, pref)
    assert n % t == 0, (n, pref)
    return t


def _layer(x, past, w, norm_mix, w_conv, norm_ffn, rel_bias):
    bsz, t, d = x.shape
    tk = KEY_TILE
    cinit = jnp.zeros((bsz, CONV_W - 1, CONV_DIM), F32) if past is None else past[3].astype(F32)
    (q, k, v, kb, vb, qi, kidx, kid, wi, sga, pc, cst) = _proj_call(
        x, cinit, norm_mix[None, :], w["wa"], w["wb"], w["wc"], w["wg"], w_conv, w["wco"], _tile(t, 256))

    if past is None:
        assert t % tk == 0
        tq, q_off, total = tk, 0, t
        n_valid_last = tk
        kid_all, kb_all, vb_all = kid, kb, vb
    else:
        p_len = past[0].shape[1]
        tq = ROW_STRIP
        assert p_len % tk == 0 and t <= tq
        q_off, total = p_len, p_len + t
        n_valid_last = t
        qpad = ((0, 0), (0, tq - t), (0, 0))
        q, qi, wi = jnp.pad(q, qpad), jnp.pad(qi, qpad), jnp.pad(wi, qpad)
        pad = ((0, 0), (0, tk - t), (0, 0))
        pk = past[0].reshape(bsz, p_len, KV_DIM).astype(BF16)
        pv = past[1].reshape(bsz, p_len, KV_DIM).astype(BF16)
        pki = past[2].astype(BF16)
        kb_all = jnp.concatenate([pk, jnp.pad(kb, pad)], axis=1)
        vb_all = jnp.concatenate([pv, jnp.pad(vb, pad)], axis=1)
        kid_all = jnp.concatenate([jnp.concatenate([pki, pki], axis=-1), jnp.pad(kid, pad)], axis=1)
    nt = kb_all.shape[1] // tk
    tiles = lambda a: a.reshape(bsz, nt, tk, LANES)
    wit = jnp.swapaxes(wi[:, :, :SUBLANES], 1, 2)
    vbt = jnp.swapaxes(tiles(vb_all), 2, 3)
    bt, btmax = _bias_tables(rel_bias, tq, tk)
    attn = _dsa_call(q, qi, wit, tiles(kid_all), tiles(kb_all), vbt, bt, btmax, tq=tq, q_off=q_off,
                     n_valid_last=n_valid_last, topk=min(TOPK_MAX, total // 4))
    attn = attn[:, :t]

    n = bsz * t
    tm = _tile(n, 512)
    x1, h2, gates = _mix_call(attn.reshape(n, ATTN_DIM), sga.reshape(n, d), pc.reshape(n, d),
                              x.reshape(n, d), w["wao"], w["wout"], norm_ffn[None, :], w["wr"], w["br"], tm)
    state = (k.reshape(bsz, t, N_KV_HEADS, HEAD_DIM), v.reshape(bsz, t, N_KV_HEADS, HEAD_DIM), kidx, cst)
    return x1, h2, gates, state


def kernel(x_prompt, x_sample, cache_k, cache_v, cache_kidx, state_conv, rel_bias, norm_mix, w_in,
           w_attn_out, w_conv, w_conv_out, w_out, norm_ffn, w_group, b_group, w_router, b_router,
           w_gate, w_up, w_down, norm_final):
    depth = w_in.shape[0]
    assert depth == 1, "the final norm is fused into the last layer's expert kernel"
    l = 0
    w = _prep_weights(w_in[l], w_attn_out[l], w_conv_out[l], w_out[l], w_group[l], b_group[l],
                      w_router[l], b_router[l], w_gate[l], w_up[l], w_down[l])
    outs = []
    states = []
    for x, past in ((x_prompt, None),
                    (x_sample, (cache_k[l], cache_v[l], cache_kidx[l], state_conv[l]))):
        bsz, t, d = x.shape
        x1, h2, gates, state = _layer(x, past, w, norm_mix[l], w_conv[l], norm_ffn[l], rel_bias)
        n = bsz * t
        y = _moe_call(h2, gates, x1, w["wgate"], w["wup"], w["wdown"], norm_final[None, :], _tile(n, 512))
        outs.append(y.reshape(bsz, t, d))
        states.append(state)
    sp, ss = states
    return (outs[0], outs[1], sp[0][None], sp[1][None], sp[2][None], sp[3][None],
            ss[0][None], ss[1][None], ss[2][None], ss[3][None])
```

```python
import functools
import math

import jax
import jax.numpy as jnp
import numpy as np
from jax import lax
from jax.experimental import pallas as pl
from jax.experimental.pallas import tpu as pltpu

F32 = jnp.float32
BF16 = jnp.bfloat16
I32 = jnp.int32

CHUNK = 64
N_HEADS = 8
N_KV_HEADS = 2
GROUP = N_HEADS // N_KV_HEADS
HEAD_DIM = 64
ATTN_DIM = N_HEADS * HEAD_DIM
KV_DIM = N_KV_HEADS * HEAD_DIM
IDX_HEADS = 4
IDX_DIM = 64
TOPK_MAX = 256
CONV_DIM = 512
CONV_W = 3
N_BUCKETS = 32
MAX_DISTANCE = 128
N_GROUPS = 4
EXPERTS_PER_GROUP = 4
N_EXPERTS = N_GROUPS * EXPERTS_PER_GROUP
EPS = 1e-6
ATTN_SCALE = HEAD_DIM ** -0.5
IDX_SCALE = IDX_DIM ** -0.5
IDX_W_SCALE = IDX_HEADS ** -0.5

LANES = 128
SUBLANES = 8
KEY_TILE = 256
ROW_STRIP = 128
MOE_EXPERTS_PER_STEP = 4
VMEM_LIMIT = 56 * 1024 * 1024
INT_MIN = -2 ** 31
IDX_BIG = 2 ** 30
F32_LOWEST = float(np.finfo(np.float32).min)
LOG2E = math.log2(math.e)
BOUND_SLACK = 1.0 + 2.0 ** -5
L_UNDERFLOW = 2.0 ** -80


def _dot(a, b):
    return jnp.dot(a, b, preferred_element_type=F32)


def _dot_nt(a, b):
    return lax.dot_general(a, b, (((1,), (1,)), ((), ())), preferred_element_type=F32)


def _sigmoid(x):
    return 1.0 / (1.0 + jnp.exp(-x))


def _rms(x, g):
    return x * lax.rsqrt(jnp.mean(x * x, axis=-1, keepdims=True) + EPS) * g


def _proj_kernel(x_ref, cinit_ref, g_ref, wa_ref, wb_ref, wc_ref, wg_ref, wconv_ref, wco_ref,
                 q_ref, k_ref, v_ref, kb_ref, vb_ref, qi_ref, kidx_ref, kid_ref, wi_ref,
                 sga_ref, pc_ref, cst_ref, carry_ref, *, tm):
    j = pl.program_id(1)
    hb = _rms(x_ref[0], g_ref[...]).astype(BF16)

    a = _dot(hb, wa_ref[...])
    q_ref[0] = a[:, :ATTN_DIM].astype(BF16)
    k = a[:, ATTN_DIM:ATTN_DIM + KV_DIM]
    v = a[:, ATTN_DIM + KV_DIM:]
    k_ref[0] = k
    v_ref[0] = v
    kb_ref[0] = k.astype(BF16)
    vb_ref[0] = v.astype(BF16)

    b = _dot(hb, wb_ref[...])
    nqi = IDX_HEADS * IDX_DIM
    qi_ref[0] = b[:, :nqi].astype(BF16)
    kid_ref[0] = b[:, nqi:nqi + LANES].astype(BF16)
    kidx_ref[0] = b[:, nqi:nqi + IDX_DIM]
    wi_ref[0] = b[:, nqi + LANES:]

    c = _dot(hb, wc_ref[...])
    cb = c[:, :CONV_DIM]
    u = c[:, CONV_DIM:2 * CONV_DIM] * c[:, 2 * CONV_DIM:]

    @pl.when(j == 0)
    def _():
        carry_ref[6:8, :] = cinit_ref[0]

    c2 = carry_ref[6:7, :]
    c1 = carry_ref[7:8, :]
    row = lax.broadcasted_iota(I32, (tm, CONV_DIM), 0)
    up1 = jnp.where(row == 0, c1, pltpu.roll(u, 1, axis=0))
    up2 = jnp.where(row == 0, c2, jnp.where(row == 1, c1, pltpu.roll(u, 2, axis=0)))
    conv = wconv_ref[0:1, :] * up2 + wconv_ref[1:2, :] * up1 + wconv_ref[2:3, :] * u
    tail = u[tm - 2:tm, :]
    carry_ref[6:8, :] = tail
    cst_ref[0] = tail
    yc = _dot((cb * conv).astype(BF16), wco_ref[...])

    gt = _dot(hb, wg_ref[...])
    d = yc.shape[-1]
    sga_ref[0] = _sigmoid(gt[:, :d])
    pc_ref[0] = _sigmoid(gt[:, d:]) * yc


def _proj_call(x, cinit, g, wa, wb, wc, wg, wconv, wco, tm):
    bsz, s, d = x.shape
    grid = (bsz, s // tm)
    tok = lambda n: pl.BlockSpec((1, tm, n), lambda b, j: (b, j, 0))
    full = lambda a: pl.BlockSpec(a.shape, lambda b, j: (0,) * a.ndim)
    st = pl.BlockSpec((1, CONV_W - 1, CONV_DIM), lambda b, j: (b, 0, 0))
    out_shape = (
        jax.ShapeDtypeStruct((bsz, s, ATTN_DIM), BF16),
        jax.ShapeDtypeStruct((bsz, s, KV_DIM), F32),
        jax.ShapeDtypeStruct((bsz, s, KV_DIM), F32),
        jax.ShapeDtypeStruct((bsz, s, KV_DIM), BF16),
        jax.ShapeDtypeStruct((bsz, s, KV_DIM), BF16),
        jax.ShapeDtypeStruct((bsz, s, IDX_HEADS * IDX_DIM), BF16),
        jax.ShapeDtypeStruct((bsz, s, IDX_DIM), F32),
        jax.ShapeDtypeStruct((bsz, s, LANES), BF16),
        jax.ShapeDtypeStruct((bsz, s, LANES), F32),
        jax.ShapeDtypeStruct((bsz, s, d), F32),
        jax.ShapeDtypeStruct((bsz, s, d), F32),
        jax.ShapeDtypeStruct((bsz, CONV_W - 1, CONV_DIM), F32),
    )
    out_specs = (tok(ATTN_DIM), tok(KV_DIM), tok(KV_DIM), tok(KV_DIM), tok(KV_DIM),
                 tok(IDX_HEADS * IDX_DIM), tok(IDX_DIM), tok(LANES), tok(LANES), tok(d), tok(d), st)
    return pl.pallas_call(
        functools.partial(_proj_kernel, tm=tm),
        grid=grid,
        in_specs=[tok(d), st, full(g), full(wa), full(wb), full(wc), full(wg), full(wconv), full(wco)],
        out_specs=out_specs,
        out_shape=out_shape,
        scratch_shapes=[pltpu.VMEM((8, CONV_DIM), F32)],
        compiler_params=pltpu.CompilerParams(
            dimension_semantics=("arbitrary", "arbitrary"), vmem_limit_bytes=VMEM_LIMIT),
        name="proj",
    )(x, cinit, g, wa, wb, wc, wg, wconv, wco)


def _tree_sum(xs):
    while len(xs) > 1:
        xs = [xs[a] + xs[a + 1] for a in range(0, len(xs) - 1, 2)] + ([xs[-1]] if len(xs) % 2 else [])
    return xs[0]


def _dsa_kernel(q_ref, qi_ref, wit_ref, kid_ref, kb_ref, vbt_ref, bt_ref, btmax_ref, out_ref,
                keys_ref, qis_ref, qs_ref, eye_ref, jcut_ref, kmax_ref, m_ref, l_ref, acc_ref,
                *, tq, tk, q_off, n_valid_last, topk, idx_bits):
    i = pl.program_id(1)
    nt = (q_off + i * tq) // tk + 1
    neg_inf = F32(-jnp.inf)

    lane = lax.broadcasted_iota(I32, (tq, LANES), 1)
    lo = lane < HEAD_DIM
    zero_b = jnp.zeros((tq, LANES), BF16)
    qi = qi_ref[0]
    for h in range(IDX_HEADS):
        blk = qi[:, (h // 2) * LANES:(h // 2 + 1) * LANES]
        qis_ref[h * tq:(h + 1) * tq, :] = jnp.where(lo if h % 2 == 0 else ~lo, blk, zero_b)
    q = q_ref[0]
    for n in range(N_KV_HEADS):
        for g in range(GROUP):
            blk = q[:, g * LANES:(g + 1) * LANES]
            qs_ref[n, g * tq:(g + 1) * tq, :] = jnp.where(lo if n == 0 else ~lo, blk, zero_b)
    eye_ref[...] = jnp.where(lax.broadcasted_iota(I32, (tq, tq), 0) == lax.broadcasted_iota(I32, (tq, tq), 1),
                             1.0, 0.0).astype(BF16)
    w_row = jnp.concatenate([wit_ref[0, h:h + 1, :] for h in range(IDX_HEADS)], axis=1)

    krow = lax.broadcasted_iota(I32, (tk, tq), 0)
    qcol = lax.broadcasted_iota(I32, (tk, tq), 1)
    lim_last = jnp.minimum(((qcol >> 6) + 1) << 6, n_valid_last)

    def score_tiles(ts):
        ss = [_dot_nt(kid_ref[0, t], qis_ref[...]) for t in ts]
        for t, s in zip(ts, ss):
            s = jnp.maximum(s, 0.0) * w_row
            sc = (s[:, 0:tq] + s[:, tq:2 * tq]) + (s[:, 2 * tq:3 * tq] + s[:, 3 * tq:4 * tq])
            sc = jnp.where(sc == 0.0, 0.0, sc)
            bits = lax.bitcast_convert_type(sc, I32)
            key = bits ^ ((bits >> 31) & 0x7FFFFFFF)
            lim = jnp.where(t == nt - 1, lim_last, tk)
            keys_ref[t] = jnp.where(krow < lim, key, INT_MIN)

    def score_body(j, carry):
        score_tiles([2 * j, 2 * j + 1])
        return carry

    lax.fori_loop(0, nt // 2, score_body, 0)

    @pl.when(nt % 2 == 1)
    def _():
        score_tiles([nt - 1])

    def count(ind_fn, cand):
        cand8 = jnp.broadcast_to(cand, (SUBLANES, tq))
        row8 = lax.broadcasted_iota(I32, (SUBLANES, tq), 0)

        def body(t, acc):
            parts = []
            for r in range(tk // SUBLANES):
                x = keys_ref[t, r * SUBLANES:(r + 1) * SUBLANES, :]
                parts.append(ind_fn(x, row8 + (t * tk + r * SUBLANES), cand8))
            return acc + _tree_sum(parts)

        acc = lax.fori_loop(0, nt, body, jnp.zeros((SUBLANES, tq), I32))
        return jnp.sum(acc.astype(F32), axis=0, keepdims=True)

    ge = lambda x, idx, c: jnp.where(x >= c, 1, 0)

    def bit_body(b, thr):
        cand = thr + (jnp.int32(1) << (31 - b))
        return jnp.where(count(ge, cand) >= topk, cand, thr)

    thr = lax.fori_loop(0, 32, bit_body, jnp.full((1, tq), INT_MIN, I32))
    thr8 = jnp.broadcast_to(thr, (SUBLANES, tq))

    jcut_ref[...] = jnp.full(jcut_ref.shape, IDX_BIG, I32)
    n_ge = count(ge, thr)

    @pl.when(jnp.max(n_ge) > topk)
    def _():
        need = topk - count(ge, thr + 1)
        tied_lt = lambda x, idx, c: jnp.where(x == thr8, jnp.where(idx < c, 1, 0), 0)

        def jbit_body(b, jj):
            cand = jj + (jnp.int32(1) << (idx_bits - 1 - b))
            return jnp.where(count(tied_lt, cand) <= need, cand, jj)

        jj = lax.fori_loop(0, idx_bits, jbit_body, jnp.zeros((1, tq), I32))
        jcut_ref[...] = jnp.broadcast_to(jj, jcut_ref.shape)

    jcut = jnp.where(thr == INT_MIN, 0, jcut_ref[0:1, :])

    @pl.when(i == 0)
    def _():
        half0 = lax.broadcasted_iota(I32, (tk, LANES), 1) < HEAD_DIM

        def kbody(t, c):
            k = kb_ref[0, t].astype(F32)
            sq = k * k
            s0 = jnp.max(jnp.sum(jnp.where(half0, sq, 0.0), axis=1, keepdims=True), axis=0, keepdims=True)
            s1 = jnp.max(jnp.sum(jnp.where(half0, 0.0, sq), axis=1, keepdims=True), axis=0, keepdims=True)
            return jnp.maximum(c[0], s0), jnp.maximum(c[1], s1)

        z = jnp.zeros((1, 1), F32)
        c = lax.fori_loop(0, kb_ref.shape[1], kbody, (z, z))
        for n in range(N_KV_HEADS):
            kmax_ref[n:n + 1, :] = jnp.broadcast_to(jnp.sqrt(c[n]), (1, LANES))

    ones8 = jnp.ones((SUBLANES, LANES), BF16)
    mb = []
    for n in range(N_KV_HEADS):
        qf = qs_ref[n].astype(F32)
        qsq = _dot_nt(ones8, (qf * qf).astype(BF16))[0:1, :]
        mb.append(BOUND_SLACK * jnp.sqrt(qsq) * kmax_ref[n:n + 1, 0:1] + btmax_ref[n, 0:1, :])

    def fold8(x, op):
        parts = [x[r * SUBLANES:(r + 1) * SUBLANES, :] for r in range(x.shape[0] // SUBLANES)]
        while len(parts) > 1:
            parts = [op(parts[a], parts[a + 1]) for a in range(0, len(parts), 2)]
        return parts[0]

    def logits(t, near, n, sel4):
        lg = _dot_nt(kb_ref[0, t], qs_ref[n]) + sel4
        if near is not None:
            lg = lg + bt_ref[near, n]
        return lg

    def selection(t):
        key = keys_ref[t]
        sel = jnp.where(key > thr, 0.0,
                        jnp.where(key == thr, jnp.where(krow + t * tk < jcut, 0.0, neg_inf), neg_inf))
        return jnp.concatenate([sel] * GROUP, axis=1)

    def tiles_fast(specs):
        sel4 = [selection(t) for t, _ in specs]
        units = [(t, near, n) for t, near in specs for n in range(N_KV_HEADS)]
        lgs = [logits(t, near, n, sel4[u // N_KV_HEADS]) for u, (t, near, n) in enumerate(units)]
        ps = [jnp.exp2(lg - mb[n]) for lg, (_, _, n) in zip(lgs, units)]
        for n in range(N_KV_HEADS):
            l_ref[n] = l_ref[n] + _tree_sum([fold8(p, jnp.add) for p, u in zip(ps, units) if u[2] == n])
        pvs = [_dot(vbt_ref[0, t, n * HEAD_DIM:(n + 1) * HEAD_DIM, :], p.astype(BF16))
               for p, (t, _, n) in zip(ps, units)]
        for n in range(N_KV_HEADS):
            acc_ref[n] = acc_ref[n] + _tree_sum([pv for pv, u in zip(pvs, units) if u[2] == n])

    def tiles_exact(specs):
        for t, near in specs:
            tile_exact(t, near)

    def tile_exact(t, near):
        sel4 = selection(t)
        vt = vbt_ref[0, t]
        for n in range(N_KV_HEADS):
            lg = logits(t, near, n, sel4)
            m_old = m_ref[n, 0:1, :]
            m_new = jnp.maximum(m_old, jnp.max(fold8(lg, jnp.maximum), axis=0, keepdims=True))
            m_safe = jnp.maximum(m_new, F32_LOWEST)
            alpha = jnp.exp2(m_old - m_safe)
            p = jnp.exp2(lg - m_safe)
            m_ref[n] = jnp.broadcast_to(m_new, m_ref.shape[1:])
            l_ref[n] = alpha * l_ref[n] + fold8(p, jnp.add)
            pv = _dot(vt[n * HEAD_DIM:(n + 1) * HEAD_DIM, :], p.astype(BF16))
            acc_ref[n] = alpha * acc_ref[n] + pv

    def run_tiles(tiles, unroll):
        l_ref[...] = jnp.zeros(l_ref.shape, F32)
        acc_ref[...] = jnp.zeros(acc_ref.shape, F32)
        n_far = jnp.maximum(nt - 2, 0)

        def far_body(j, carry):
            tiles([(j * unroll + u, None) for u in range(unroll)])
            return carry

        lax.fori_loop(0, n_far // unroll, far_body, 0)
        for u in range(unroll - 1):
            @pl.when(n_far % unroll > u)
            def _():
                tiles([((n_far // unroll) * unroll + u, None)])

        @pl.when(nt >= 2)
        def _():
            tiles([(nt - 2, 0), (nt - 1, 1)])

        @pl.when(nt < 2)
        def _():
            tiles([(nt - 1, 1)])

    run_tiles(tiles_fast, 2)
    l_min = jnp.min(jnp.minimum(jnp.sum(l_ref[0], axis=0, keepdims=True),
                                jnp.sum(l_ref[1], axis=0, keepdims=True)))

    @pl.when(jnp.logical_not(l_min >= L_UNDERFLOW))
    def _():
        m_ref[...] = jnp.full(m_ref.shape, neg_inf, F32)
        run_tiles(tiles_exact, 1)

    o = [acc_ref[n] / jnp.sum(l_ref[n], axis=0, keepdims=True) for n in range(N_KV_HEADS)]
    for g in range(GROUP):
        pair = jnp.concatenate([o[n][:, g * tq:(g + 1) * tq] for n in range(N_KV_HEADS)], axis=0)
        out_ref[0, :, g * LANES:(g + 1) * LANES] = _dot_nt(eye_ref[...], pair.astype(BF16)).astype(BF16)


def _dsa_call(q, qi, wit, kid, kb, vb, bt, btmax, *, tq, q_off, n_valid_last, topk):
    bsz, sq, _ = q.shape
    nt_max, tk = kid.shape[1], kid.shape[2]
    idx_bits = int(nt_max * tk).bit_length()
    r4 = GROUP * tq
    tok = lambda n: pl.BlockSpec((1, tq, n), lambda b, i: (b, i, 0))
    keys = pl.BlockSpec((1, nt_max, tk, LANES), lambda b, i: (b, 0, 0, 0))
    kern = functools.partial(_dsa_kernel, tq=tq, tk=tk, q_off=q_off, n_valid_last=n_valid_last,
                             topk=topk, idx_bits=idx_bits)
    return pl.pallas_call(
        kern,
        grid=(bsz, sq // tq),
        in_specs=[tok(ATTN_DIM), tok(IDX_HEADS * IDX_DIM),
                  pl.BlockSpec((1, SUBLANES, tq), lambda b, i: (b, 0, i)), keys, keys,
                  pl.BlockSpec((1, nt_max, KV_DIM, tk), lambda b, i: (b, 0, 0, 0)),
                  pl.BlockSpec(bt.shape, lambda b, i: (0, 0, 0, 0)),
                  pl.BlockSpec(btmax.shape, lambda b, i: (0, 0, 0))],
        out_specs=tok(ATTN_DIM),
        out_shape=jax.ShapeDtypeStruct((bsz, sq, ATTN_DIM), BF16),
        scratch_shapes=[
            pltpu.VMEM((nt_max, tk, tq), I32),
            pltpu.VMEM((IDX_HEADS * tq, LANES), BF16),
            pltpu.VMEM((N_KV_HEADS, r4, LANES), BF16),
            pltpu.VMEM((tq, tq), BF16),
            pltpu.VMEM((SUBLANES, tq), I32),
            pltpu.VMEM((SUBLANES, LANES), F32),
            pltpu.VMEM((N_KV_HEADS, SUBLANES, r4), F32),
            pltpu.VMEM((N_KV_HEADS, SUBLANES, r4), F32),
            pltpu.VMEM((N_KV_HEADS, HEAD_DIM, r4), F32),
        ],
        compiler_params=pltpu.CompilerParams(
            dimension_semantics=("arbitrary", "arbitrary"), vmem_limit_bytes=VMEM_LIMIT),
        name="dsa",
    )(q, qi, wit, kid, kb, vb, bt, btmax)


def _t5_bucket(rel):
    nb = N_BUCKETS // 2
    max_exact = nb // 2
    ret = jnp.where(rel > 0, nb, 0)
    n = jnp.abs(rel)
    nf = jnp.maximum(n, 1).astype(F32)
    large = max_exact + (jnp.log(nf / max_exact) / math.log(MAX_DISTANCE / max_exact)
                         * (nb - max_exact)).astype(I32)
    large = jnp.minimum(large, nb - 1)
    return ret + jnp.where(n < max_exact, n, large)


def _bias_tables(rel_bias, tq, tk):
    far = rel_bias[_t5_bucket(jnp.full((1,), -(tk + 1), I32))[0]]
    period = 2 * tk + tq
    off = jnp.arange(period, dtype=I32)
    off = jnp.where(off < 2 * tk, off, off - period)
    diag = ((rel_bias[_t5_bucket(off - tk)] - far) * LOG2E).T
    tab = jnp.tile(diag, (1, tq))[:, :tq * (period - 1)].reshape(N_HEADS, tq, period - 1)[:, :, :2 * tk]
    tab = tab.reshape(N_KV_HEADS, GROUP, tq, 2, tk).transpose(3, 0, 4, 1, 2)
    tab = tab.reshape(2, N_KV_HEADS, tk, GROUP * tq).astype(F32)
    tab_max = jnp.maximum(jnp.max(tab, axis=(0, 2)), 0.0)
    return tab, jnp.broadcast_to(tab_max[:, None, :], (N_KV_HEADS, SUBLANES, GROUP * tq))


def _mix_kernel(attn_ref, sga_ref, pc_ref, x_ref, wao_ref, wout_ref, g_ref, wr_ref, br_ref,
                x1_ref, h2_ref, gates_ref):
    n_sub = 2 if attn_ref.shape[0] % (2 * SUBLANES * 2) == 0 else 1
    ts = attn_ref.shape[0] // n_sub
    subs = [slice(k * ts, (k + 1) * ts) for k in range(n_sub)]
    yas = [_dot(attn_ref[r, :], wao_ref[...]) for r in subs]
    merged = [(sga_ref[r, :] * ya + pc_ref[r, :]).astype(BF16) for r, ya in zip(subs, yas)]
    mixes = [_dot(m, wout_ref[...]) for m in merged]
    x1s = [x_ref[r, :] + mix for r, mix in zip(subs, mixes)]
    hbs = [_rms(x1, g_ref[...]).astype(BF16) for x1 in x1s]
    rls = [_dot(hb, wr_ref[...]) for hb in hbs]
    for r, x1, hb, rl in zip(subs, x1s, hbs, rls):
        x1_ref[r, :] = x1
        h2_ref[r, :] = hb
        gates_ref[r, :] = _route(rl, br_ref)


def _route(rl, br_ref):
    tm = rl.shape[0]
    lane = lax.broadcasted_iota(I32, (tm, LANES), 1)
    neg_inf = F32(-jnp.inf)
    big = I32(LANES)

    def first_argmax(x):
        mx = jnp.max(x, axis=-1, keepdims=True)
        return jnp.min(jnp.where(x == mx, lane, big), axis=-1, keepdims=True)

    def pick(x, idx):
        return jnp.sum(jnp.where(lane == idx, x, 0.0), axis=-1, keepdims=True)

    glog = rl[:, :LANES]
    gvalid = lane < N_GROUPS
    g_sel = first_argmax(jnp.where(gvalid, glog + br_ref[0:1, :], neg_inf))
    gm = jnp.max(jnp.where(gvalid, glog, neg_inf), axis=-1, keepdims=True)
    gexp = jnp.where(gvalid, jnp.exp(glog - gm), 0.0)
    g_prob = pick(gexp, g_sel) / jnp.sum(gexp, axis=-1, keepdims=True)

    elog = rl[:, LANES:]
    in_grp = (lane // EXPERTS_PER_GROUP) == g_sel
    eb = jnp.where(in_grp, elog + br_ref[1:2, :], neg_inf)
    i1 = first_argmax(eb)
    i2 = first_argmax(jnp.where(lane == i1, neg_inf, eb))
    em = jnp.max(jnp.where(in_grp, elog, neg_inf), axis=-1, keepdims=True)
    eexp = jnp.where(in_grp, jnp.exp(elog - em), 0.0)
    esum = jnp.sum(eexp, axis=-1, keepdims=True)
    p1 = pick(eexp, i1) / esum
    p2 = pick(eexp, i2) / esum
    w1 = g_prob * p1 / (p1 + p2)
    w2 = g_prob * p2 / (p1 + p2)
    return jnp.where(lane == i1, w1, 0.0) + jnp.where(lane == i2, w2, 0.0)


def _mix_call(attn, sga, pc, x, wao, wout, g, wr, br, tm):
    n, d = x.shape
    tok = lambda c: pl.BlockSpec((tm, c), lambda i: (i, 0))
    full = lambda a: pl.BlockSpec(a.shape, lambda i: (0,) * a.ndim)
    return pl.pallas_call(
        _mix_kernel,
        grid=(n // tm,),
        in_specs=[tok(ATTN_DIM), tok(d), tok(d), tok(d), full(wao), full(wout), full(g), full(wr), full(br)],
        out_specs=(tok(d), tok(d), tok(LANES)),
        out_shape=(jax.ShapeDtypeStruct((n, d), F32), jax.ShapeDtypeStruct((n, d), BF16),
                   jax.ShapeDtypeStruct((n, LANES), F32)),
        compiler_params=pltpu.CompilerParams(
            dimension_semantics=("arbitrary",), vmem_limit_bytes=VMEM_LIMIT),
        name="mix",
    )(attn, sga, pc, x, wao, wout, g, wr, br)


def _moe_kernel(h_ref, gates_ref, x1_ref, wg_ref, wu_ref, wd_ref, g_ref, out_ref, acc_ref):
    e = pl.program_id(1)

    @pl.when(e == 0)
    def _():
        acc_ref[...] = jnp.zeros(acc_ref.shape, F32)

    hb = h_ref[...]
    n_e = wg_ref.shape[0]
    ups = [(_dot(hb, wg_ref[k]), _dot(hb, wu_ref[k])) for k in range(n_e)]
    acts = [(a * _sigmoid(a) * b).astype(BF16) for a, b in ups]
    ys = [_dot(act, wd_ref[k]) for k, act in enumerate(acts)]
    lane = lax.broadcasted_iota(I32, gates_ref.shape, 1)
    gates = gates_ref[...]
    acc = acc_ref[...]
    for k, y in enumerate(ys):
        ge = jnp.sum(jnp.where(lane == e * n_e + k, gates, 0.0), axis=-1, keepdims=True)
        acc = acc + ge * y
    acc_ref[...] = acc

    @pl.when(e == pl.num_programs(1) - 1)
    def _():
        out_ref[...] = _rms(x1_ref[...] + acc_ref[...], g_ref[...])


def _moe_call(hb, gates, x1, wg, wu, wd, g, tm):
    n, d = x1.shape
    ne, _, de = wg.shape
    tok = lambda c: pl.BlockSpec((tm, c), lambda i, e: (i, 0))
    per = MOE_EXPERTS_PER_STEP
    return pl.pallas_call(
        _moe_kernel,
        grid=(n // tm, ne // per),
        in_specs=[tok(d), tok(LANES), tok(d),
                  pl.BlockSpec((per, d, de), lambda i, e: (e, 0, 0)),
                  pl.BlockSpec((per, d, de), lambda i, e: (e, 0, 0)),
                  pl.BlockSpec((per, de, d), lambda i, e: (e, 0, 0)),
                  pl.BlockSpec(g.shape, lambda i, e: (0, 0))],
        out_specs=tok(d),
        out_shape=jax.ShapeDtypeStruct((n, d), F32),
        scratch_shapes=[pltpu.VMEM((tm, d), F32)],
        compiler_params=pltpu.CompilerParams(
            dimension_semantics=("arbitrary", "arbitrary"), vmem_limit_bytes=VMEM_LIMIT),
        name="moe",
    )(hb, gates, x1, wg, wu, wd, g)


def _head_pair_perm():
    cols = []
    for g in range(GROUP):
        for n in range(N_KV_HEADS):
            h = n * GROUP + g
            cols.extend(range(h * HEAD_DIM, (h + 1) * HEAD_DIM))
    return np.asarray(cols, np.int32)


def _prep_weights(w_in, w_attn_out, w_conv_out, w_out, w_group, b_group, w_router, b_router,
                  w_gate, w_up, w_down):
    d = w_in.shape[0]
    o = np.cumsum([0, ATTN_DIM, KV_DIM, KV_DIM, IDX_HEADS * IDX_DIM, IDX_DIM, IDX_HEADS,
                   CONV_DIM, CONV_DIM, CONV_DIM, d, d])
    perm = _head_pair_perm()
    wq = w_in[:, o[0]:o[1]][:, perm] * (ATTN_SCALE * LOG2E)
    wa = jnp.concatenate([wq, w_in[:, o[1]:o[3]]], axis=1).astype(BF16)
    wki = w_in[:, o[4]:o[5]]
    wwi = jnp.pad(w_in[:, o[5]:o[6]] * IDX_W_SCALE, ((0, 0), (0, LANES - IDX_HEADS)))
    wb = jnp.concatenate([w_in[:, o[3]:o[4]] * IDX_SCALE, wki, wki, wwi], axis=1).astype(BF16)
    wc = w_in[:, o[6]:o[9]].astype(BF16)
    wg = w_in[:, o[9]:o[11]].astype(BF16)
    wr = jnp.concatenate([jnp.pad(w_group, ((0, 0), (0, LANES - N_GROUPS))),
                          jnp.pad(w_router, ((0, 0), (0, LANES - N_EXPERTS)))], axis=1).astype(BF16)
    br = jnp.stack([jnp.pad(b_group, (0, LANES - N_GROUPS)),
                    jnp.pad(b_router, (0, LANES - N_EXPERTS))]).astype(F32)
    return dict(wa=wa, wb=wb, wc=wc, wg=wg, wao=w_attn_out[perm, :].astype(BF16),
                wco=w_conv_out.astype(BF16), wout=w_out.astype(BF16), wr=wr, br=br,
                wgate=w_gate.astype(BF16), wup=w_up.astype(BF16), wdown=w_down.astype(BF16))


def _tile(n, pref):
    t = min(n, pref)
    assert n % t == 0, (n, pref)
    return t


def _layer(x, past, w, norm_mix, w_conv, norm_ffn, rel_bias):
    bsz, t, d = x.shape
    tk = KEY_TILE
    cinit = jnp.zeros((bsz, CONV_W - 1, CONV_DIM), F32) if past is None else past[3].astype(F32)
    (q, k, v, kb, vb, qi, kidx, kid, wi, sga, pc, cst) = _proj_call(
        x, cinit, norm_mix[None, :], w["wa"], w["wb"], w["wc"], w["wg"], w_conv, w["wco"], _tile(t, 512))

    if past is None:
        assert t % tk == 0
        tq, q_off, total = tk, 0, t
        n_valid_last = tk
        kid_all, kb_all, vb_all = kid, kb, vb
    else:
        p_len = past[0].shape[1]
        tq = ROW_STRIP
        assert p_len % tk == 0 and t <= tq
        q_off, total = p_len, p_len + t
        n_valid_last = t
        qpad = ((0, 0), (0, tq - t), (0, 0))
        q, qi, wi = jnp.pad(q, qpad), jnp.pad(qi, qpad), jnp.pad(wi, qpad)
        pad = ((0, 0), (0, tk - t), (0, 0))
        pk = past[0].reshape(bsz, p_len, KV_DIM).astype(BF16)
        pv = past[1].reshape(bsz, p_len, KV_DIM).astype(BF16)
        pki = past[2].astype(BF16)
        kb_all = jnp.concatenate([pk, jnp.pad(kb, pad)], axis=1)
        vb_all = jnp.concatenate([pv, jnp.pad(vb, pad)], axis=1)
        kid_all = jnp.concatenate([jnp.concatenate([pki, pki], axis=-1), jnp.pad(kid, pad)], axis=1)
    nt = kb_all.shape[1] // tk
    tiles = lambda a: a.reshape(bsz, nt, tk, LANES)
    wit = jnp.swapaxes(wi[:, :, :SUBLANES], 1, 2)
    vbt = jnp.swapaxes(tiles(vb_all), 2, 3)
    bt, btmax = _bias_tables(rel_bias, tq, tk)
    attn = _dsa_call(q, qi, wit, tiles(kid_all), tiles(kb_all), vbt, bt, btmax, tq=tq, q_off=q_off,
                     n_valid_last=n_valid_last, topk=min(TOPK_MAX, total // 4))
    attn = attn[:, :t]

    n = bsz * t
    tm = _tile(n, 512)
    x1, h2, gates = _mix_call(attn.reshape(n, ATTN_DIM), sga.reshape(n, d), pc.reshape(n, d),
                              x.reshape(n, d), w["wao"], w["wout"], norm_ffn[None, :], w["wr"], w["br"], tm)
    state = (k.reshape(bsz, t, N_KV_HEADS, HEAD_DIM), v.reshape(bsz, t, N_KV_HEADS, HEAD_DIM), kidx, cst)
    return x1, h2, gates, state


def kernel(x_prompt, x_sample, cache_k, cache_v, cache_kidx, state_conv, rel_bias, norm_mix, w_in,
           w_attn_out, w_conv, w_conv_out, w_out, norm_ffn, w_group, b_group, w_router, b_router,
           w_gate, w_up, w_down, norm_final):
    depth = w_in.shape[0]
    assert depth == 1, "the final norm is fused into the last layer's expert kernel"
    l = 0
    w = _prep_weights(w_in[l], w_attn_out[l], w_conv_out[l], w_out[l], w_group[l], b_group[l],
                      w_router[l], b_router[l], w_gate[l], w_up[l], w_down[l])
    outs = []
    states = []
    for x, past in ((x_prompt, None),
                    (x_sample, (cache_k[l], cache_v[l], cache_kidx[l], state_conv[l]))):
        bsz, t, d = x.shape
        x1, h2, gates, state = _layer(x, past, w, norm_mix[l], w_conv[l], norm_ffn[l], rel_bias)
        n = bsz * t
        y = _moe_call(h2, gates, x1, w["wgate"], w["wup"], w["wdown"], norm_final[None, :], _tile(n, 512))
        outs.append(y.reshape(bsz, t, d))
        states.append(state)
    sp, ss = states
    return (outs[0], outs[1], sp[0][None], sp[1][None], sp[2][None], sp[3][None],
            ss[0][None], ss[1][None], ss[2][None], ss[3][None])
```

```python
import functools
import math

import jax
import jax.numpy as jnp
import numpy as np
from jax import lax
from jax.experimental import pallas as pl
from jax.experimental.pallas import tpu as pltpu

F32 = jnp.float32
BF16 = jnp.bfloat16
I32 = jnp.int32

CHUNK = 64
N_HEADS = 8
N_KV_HEADS = 2
GROUP = N_HEADS // N_KV_HEADS
HEAD_DIM = 64
ATTN_DIM = N_HEADS * HEAD_DIM
KV_DIM = N_KV_HEADS * HEAD_DIM
IDX_HEADS = 4
IDX_DIM = 64
TOPK_MAX = 256
CONV_DIM = 512
CONV_W = 3
N_BUCKETS = 32
MAX_DISTANCE = 128
N_GROUPS = 4
EXPERTS_PER_GROUP = 4
N_EXPERTS = N_GROUPS * EXPERTS_PER_GROUP
EPS = 1e-6
ATTN_SCALE = HEAD_DIM ** -0.5
IDX_SCALE = IDX_DIM ** -0.5
IDX_W_SCALE = IDX_HEADS ** -0.5

LANES = 128
SUBLANES = 8
KEY_TILE = 256
ROW_STRIP = 128
MOE_EXPERTS_PER_STEP = 4
VMEM_LIMIT = 56 * 1024 * 1024
INT_MIN = -2 ** 31
IDX_BIG = 2 ** 30
F32_LOWEST = float(np.finfo(np.float32).min)
LOG2E = math.log2(math.e)
BOUND_SLACK = 1.0 + 2.0 ** -5
L_UNDERFLOW = 2.0 ** -80


def _dot(a, b):
    return jnp.dot(a, b, preferred_element_type=F32)


def _dot_nt(a, b):
    return lax.dot_general(a, b, (((1,), (1,)), ((), ())), preferred_element_type=F32)


def _sigmoid(x):
    return 1.0 / (1.0 + jnp.exp(-x))


def _rms(x, g):
    return x * lax.rsqrt(jnp.mean(x * x, axis=-1, keepdims=True) + EPS) * g


def _proj_kernel(x_ref, cinit_ref, g_ref, wa_ref, wb_ref, wc_ref, wg_ref, wconv_ref, wco_ref,
                 q_ref, k_ref, v_ref, kb_ref, vb_ref, qi_ref, kidx_ref, kid_ref, wi_ref,
                 sga_ref, pc_ref, cst_ref, carry_ref, *, tm):
    j = pl.program_id(1)
    hb = _rms(x_ref[0], g_ref[...]).astype(BF16)

    a = _dot(hb, wa_ref[...])
    q_ref[0] = a[:, :ATTN_DIM].astype(BF16)
    k = a[:, ATTN_DIM:ATTN_DIM + KV_DIM]
    v = a[:, ATTN_DIM + KV_DIM:]
    k_ref[0] = k
    v_ref[0] = v
    kb_ref[0] = k.astype(BF16)
    vb_ref[0] = v.astype(BF16)

    b = _dot(hb, wb_ref[...])
    nqi = IDX_HEADS * IDX_DIM
    qi_ref[0] = b[:, :nqi].astype(BF16)
    kid_ref[0] = b[:, nqi:nqi + LANES].astype(BF16)
    kidx_ref[0] = b[:, nqi:nqi + IDX_DIM]
    wi_ref[0] = b[:, nqi + LANES:]

    c = _dot(hb, wc_ref[...])
    cb = c[:, :CONV_DIM]
    u = c[:, CONV_DIM:2 * CONV_DIM] * c[:, 2 * CONV_DIM:]

    @pl.when(j == 0)
    def _():
        carry_ref[6:8, :] = cinit_ref[0]

    c2 = carry_ref[6:7, :]
    c1 = carry_ref[7:8, :]
    row = lax.broadcasted_iota(I32, (tm, CONV_DIM), 0)
    up1 = jnp.where(row == 0, c1, pltpu.roll(u, 1, axis=0))
    up2 = jnp.where(row == 0, c2, jnp.where(row == 1, c1, pltpu.roll(u, 2, axis=0)))
    conv = wconv_ref[0:1, :] * up2 + wconv_ref[1:2, :] * up1 + wconv_ref[2:3, :] * u
    tail = u[tm - 2:tm, :]
    carry_ref[6:8, :] = tail
    cst_ref[0] = tail
    yc = _dot((cb * conv).astype(BF16), wco_ref[...])

    gt = _dot(hb, wg_ref[...])
    d = yc.shape[-1]
    sga_ref[0] = _sigmoid(gt[:, :d])
    pc_ref[0] = _sigmoid(gt[:, d:]) * yc


def _proj_call(x, cinit, g, wa, wb, wc, wg, wconv, wco, tm):
    bsz, s, d = x.shape
    grid = (bsz, s // tm)
    tok = lambda n: pl.BlockSpec((1, tm, n), lambda b, j: (b, j, 0))
    full = lambda a: pl.BlockSpec(a.shape, lambda b, j: (0,) * a.ndim)
    st = pl.BlockSpec((1, CONV_W - 1, CONV_DIM), lambda b, j: (b, 0, 0))
    out_shape = (
        jax.ShapeDtypeStruct((bsz, s, ATTN_DIM), BF16),
        jax.ShapeDtypeStruct((bsz, s, KV_DIM), F32),
        jax.ShapeDtypeStruct((bsz, s, KV_DIM), F32),
        jax.ShapeDtypeStruct((bsz, s, KV_DIM), BF16),
        jax.ShapeDtypeStruct((bsz, s, KV_DIM), BF16),
        jax.ShapeDtypeStruct((bsz, s, IDX_HEADS * IDX_DIM), BF16),
        jax.ShapeDtypeStruct((bsz, s, IDX_DIM), F32),
        jax.ShapeDtypeStruct((bsz, s, LANES), BF16),
        jax.ShapeDtypeStruct((bsz, s, LANES), F32),
        jax.ShapeDtypeStruct((bsz, s, d), F32),
        jax.ShapeDtypeStruct((bsz, s, d), F32),
        jax.ShapeDtypeStruct((bsz, CONV_W - 1, CONV_DIM), F32),
    )
    out_specs = (tok(ATTN_DIM), tok(KV_DIM), tok(KV_DIM), tok(KV_DIM), tok(KV_DIM),
                 tok(IDX_HEADS * IDX_DIM), tok(IDX_DIM), tok(LANES), tok(LANES), tok(d), tok(d), st)
    return pl.pallas_call(
        functools.partial(_proj_kernel, tm=tm),
        grid=grid,
        in_specs=[tok(d), st, full(g), full(wa), full(wb), full(wc), full(wg), full(wconv), full(wco)],
        out_specs=out_specs,
        out_shape=out_shape,
        scratch_shapes=[pltpu.VMEM((8, CONV_DIM), F32)],
        compiler_params=pltpu.CompilerParams(
            dimension_semantics=("arbitrary", "arbitrary"), vmem_limit_bytes=VMEM_LIMIT),
        name="proj",
    )(x, cinit, g, wa, wb, wc, wg, wconv, wco)


def _tree_sum(xs):
    while len(xs) > 1:
        xs = [xs[a] + xs[a + 1] for a in range(0, len(xs) - 1, 2)] + ([xs[-1]] if len(xs) % 2 else [])
    return xs[0]


def _dsa_kernel(q_ref, qi_ref, wit_ref, kid_ref, kb_ref, vbt_ref, bt_ref, btmax_ref, out_ref,
                keys_ref, qis_ref, qs_ref, eye_ref, jcut_ref, kmax_ref, m_ref, l_ref, acc_ref,
                *, tq, tk, q_off, n_valid_last, topk, idx_bits):
    i = pl.program_id(1)
    nt = (q_off + i * tq) // tk + 1
    neg_inf = F32(-jnp.inf)

    lane = lax.broadcasted_iota(I32, (tq, LANES), 1)
    lo = lane < HEAD_DIM
    zero_b = jnp.zeros((tq, LANES), BF16)
    qi = qi_ref[0]
    for h in range(IDX_HEADS):
        blk = qi[:, (h // 2) * LANES:(h // 2 + 1) * LANES]
        qis_ref[h * tq:(h + 1) * tq, :] = jnp.where(lo if h % 2 == 0 else ~lo, blk, zero_b)
    q = q_ref[0]
    for n in range(N_KV_HEADS):
        for g in range(GROUP):
            blk = q[:, g * LANES:(g + 1) * LANES]
            qs_ref[n, g * tq:(g + 1) * tq, :] = jnp.where(lo if n == 0 else ~lo, blk, zero_b)
    eye_ref[...] = jnp.where(lax.broadcasted_iota(I32, (tq, tq), 0) == lax.broadcasted_iota(I32, (tq, tq), 1),
                             1.0, 0.0).astype(BF16)
    w_row = jnp.concatenate([wit_ref[0, h:h + 1, :] for h in range(IDX_HEADS)], axis=1)

    krow = lax.broadcasted_iota(I32, (tk, tq), 0)
    qcol = lax.broadcasted_iota(I32, (tk, tq), 1)
    lim_last = jnp.minimum(((qcol >> 6) + 1) << 6, n_valid_last)

    def score_tiles(ts):
        ss = [_dot_nt(kid_ref[0, t], qis_ref[...]) for t in ts]
        for t, s in zip(ts, ss):
            s = jnp.maximum(s, 0.0) * w_row
            sc = (s[:, 0:tq] + s[:, tq:2 * tq]) + (s[:, 2 * tq:3 * tq] + s[:, 3 * tq:4 * tq])
            sc = jnp.where(sc == 0.0, 0.0, sc)
            bits = lax.bitcast_convert_type(sc, I32)
            key = bits ^ ((bits >> 31) & 0x7FFFFFFF)
            lim = jnp.where(t == nt - 1, lim_last, tk)
            keys_ref[t] = jnp.where(krow < lim, key, INT_MIN)

    def score_body(j, carry):
        score_tiles([2 * j, 2 * j + 1])
        return carry

    lax.fori_loop(0, nt // 2, score_body, 0)

    @pl.when(nt % 2 == 1)
    def _():
        score_tiles([nt - 1])

    @pl.when(nt % 2 == 1)
    def _():
        keys_ref[nt] = jnp.full((tk, tq), INT_MIN, I32)

    def count(ind_fn, cand):
        cand8 = jnp.broadcast_to(cand, (SUBLANES, tq))
        row8 = lax.broadcasted_iota(I32, (SUBLANES, tq), 0)

        def body(j, acc):
            for t in (2 * j, 2 * j + 1):
                parts = []
                for r in range(tk // SUBLANES):
                    x = keys_ref[t, r * SUBLANES:(r + 1) * SUBLANES, :]
                    parts.append(ind_fn(x, row8 + (t * tk + r * SUBLANES), cand8))
                acc = acc + _tree_sum(parts)
            return acc

        acc = lax.fori_loop(0, (nt + 1) // 2, body, jnp.zeros((SUBLANES, tq), I32))
        return jnp.sum(acc.astype(F32), axis=0, keepdims=True)

    ge = lambda x, idx, c: jnp.where(x >= c, 1, 0)

    def bit_body(b, thr):
        cand = thr + (jnp.int32(1) << (31 - b))
        return jnp.where(count(ge, cand) >= topk, cand, thr)

    thr = lax.fori_loop(0, 32, bit_body, jnp.full((1, tq), INT_MIN, I32))
    thr8 = jnp.broadcast_to(thr, (SUBLANES, tq))

    jcut_ref[...] = jnp.full(jcut_ref.shape, IDX_BIG, I32)
    n_ge = count(ge, thr)

    @pl.when(jnp.max(n_ge) > topk)
    def _():
        need = topk - count(ge, thr + 1)
        tied_lt = lambda x, idx, c: jnp.where(x == thr8, jnp.where(idx < c, 1, 0), 0)

        def jbit_body(b, jj):
            cand = jj + (jnp.int32(1) << (idx_bits - 1 - b))
            return jnp.where(count(tied_lt, cand) <= need, cand, jj)

        jj = lax.fori_loop(0, idx_bits, jbit_body, jnp.zeros((1, tq), I32))
        jcut_ref[...] = jnp.broadcast_to(jj, jcut_ref.shape)

    jcut = jnp.where(thr == INT_MIN, 0, jcut_ref[0:1, :])

    @pl.when(i == 0)
    def _():
        half0 = lax.broadcasted_iota(I32, (tk, LANES), 1) < HEAD_DIM

        def kbody(t, c):
            k = kb_ref[0, t].astype(F32)
            sq = k * k
            s0 = jnp.max(jnp.sum(jnp.where(half0, sq, 0.0), axis=1, keepdims=True), axis=0, keepdims=True)
            s1 = jnp.max(jnp.sum(jnp.where(half0, 0.0, sq), axis=1, keepdims=True), axis=0, keepdims=True)
            return jnp.maximum(c[0], s0), jnp.maximum(c[1], s1)

        z = jnp.zeros((1, 1), F32)
        c = lax.fori_loop(0, kb_ref.shape[1], kbody, (z, z))
        for n in range(N_KV_HEADS):
            kmax_ref[n:n + 1, :] = jnp.broadcast_to(jnp.sqrt(c[n]), (1, LANES))

    ones8 = jnp.ones((SUBLANES, LANES), BF16)
    mb = []
    for n in range(N_KV_HEADS):
        qf = qs_ref[n].astype(F32)
        qsq = _dot_nt(ones8, (qf * qf).astype(BF16))[0:1, :]
        mb.append(BOUND_SLACK * jnp.sqrt(qsq) * kmax_ref[n:n + 1, 0:1] + btmax_ref[n, 0:1, :])

    def fold8(x, op):
        parts = [x[r * SUBLANES:(r + 1) * SUBLANES, :] for r in range(x.shape[0] // SUBLANES)]
        while len(parts) > 1:
            parts = [op(parts[a], parts[a + 1]) for a in range(0, len(parts), 2)]
        return parts[0]

    def logits(t, near, n, sel4):
        del near
        lg = _dot_nt(kb_ref[0, t], qs_ref[n]) + sel4
        return lg + bt_ref[jnp.clip(t - (nt - 3), 0, 2), n]

    def selection(t):
        key = keys_ref[t]
        sel = jnp.where(key > thr, 0.0,
                        jnp.where(key == thr, jnp.where(krow + t * tk < jcut, 0.0, neg_inf), neg_inf))
        return jnp.concatenate([sel] * GROUP, axis=1)

    def tiles_fast(specs):
        sel4 = [selection(t) for t, _ in specs]
        units = [(t, near, n) for t, near in specs for n in range(N_KV_HEADS)]
        lgs = [logits(t, near, n, sel4[u // N_KV_HEADS]) for u, (t, near, n) in enumerate(units)]
        ps = [jnp.exp2(lg - mb[n]) for lg, (_, _, n) in zip(lgs, units)]
        for n in range(N_KV_HEADS):
            l_ref[n] = l_ref[n] + _tree_sum([fold8(p, jnp.add) for p, u in zip(ps, units) if u[2] == n])
        pvs = [_dot(vbt_ref[0, t, n * HEAD_DIM:(n + 1) * HEAD_DIM, :], p.astype(BF16))
               for p, (t, _, n) in zip(ps, units)]
        for n in range(N_KV_HEADS):
            acc_ref[n] = acc_ref[n] + _tree_sum([pv for pv, u in zip(pvs, units) if u[2] == n])

    def tiles_exact(specs):
        for t, near in specs:
            tile_exact(t, near)

    def tile_exact(t, near):
        sel4 = selection(t)
        vt = vbt_ref[0, t]
        for n in range(N_KV_HEADS):
            lg = logits(t, near, n, sel4)
            m_old = m_ref[n, 0:1, :]
            m_new = jnp.maximum(m_old, jnp.max(fold8(lg, jnp.maximum), axis=0, keepdims=True))
            m_safe = jnp.maximum(m_new, F32_LOWEST)
            alpha = jnp.exp2(m_old - m_safe)
            p = jnp.exp2(lg - m_safe)
            m_ref[n] = jnp.broadcast_to(m_new, m_ref.shape[1:])
            l_ref[n] = alpha * l_ref[n] + fold8(p, jnp.add)
            pv = _dot(vt[n * HEAD_DIM:(n + 1) * HEAD_DIM, :], p.astype(BF16))
            acc_ref[n] = alpha * acc_ref[n] + pv

    def run_tiles(tiles, unroll):
        l_ref[...] = jnp.zeros(l_ref.shape, F32)
        acc_ref[...] = jnp.zeros(acc_ref.shape, F32)

        def body(j, carry):
            tiles([(j * unroll + u, None) for u in range(unroll)])
            return carry

        lax.fori_loop(0, nt // unroll, body, 0)
        for u in range(unroll - 1):
            @pl.when(nt % unroll > u)
            def _():
                tiles([((nt // unroll) * unroll + u, None)])

    run_tiles(tiles_fast, 2)
    l_min = jnp.min(jnp.minimum(jnp.sum(l_ref[0], axis=0, keepdims=True),
                                jnp.sum(l_ref[1], axis=0, keepdims=True)))

    @pl.when(jnp.logical_not(l_min >= L_UNDERFLOW))
    def _():
        m_ref[...] = jnp.full(m_ref.shape, neg_inf, F32)
        run_tiles(tiles_exact, 1)

    o = [acc_ref[n] / jnp.sum(l_ref[n], axis=0, keepdims=True) for n in range(N_KV_HEADS)]
    for g in range(GROUP):
        pair = jnp.concatenate([o[n][:, g * tq:(g + 1) * tq] for n in range(N_KV_HEADS)], axis=0)
        out_ref[0, :, g * LANES:(g + 1) * LANES] = _dot_nt(eye_ref[...], pair.astype(BF16)).astype(BF16)


def _dsa_call(q, qi, wit, kid, kb, vb, bt, btmax, *, tq, q_off, n_valid_last, topk):
    bsz, sq, _ = q.shape
    nt_max, tk = kid.shape[1], kid.shape[2]
    idx_bits = int(nt_max * tk).bit_length()
    r4 = GROUP * tq
    tok = lambda n: pl.BlockSpec((1, tq, n), lambda b, i: (b, i, 0))
    keys = pl.BlockSpec((1, nt_max, tk, LANES), lambda b, i: (b, 0, 0, 0))
    kern = functools.partial(_dsa_kernel, tq=tq, tk=tk, q_off=q_off, n_valid_last=n_valid_last,
                             topk=topk, idx_bits=idx_bits)
    return pl.pallas_call(
        kern,
        grid=(bsz, sq // tq),
        in_specs=[tok(ATTN_DIM), tok(IDX_HEADS * IDX_DIM),
                  pl.BlockSpec((1, SUBLANES, tq), lambda b, i: (b, 0, i)), keys, keys,
                  pl.BlockSpec((1, nt_max, KV_DIM, tk), lambda b, i: (b, 0, 0, 0)),
                  pl.BlockSpec(bt.shape, lambda b, i: (0, 0, 0, 0)),
                  pl.BlockSpec(btmax.shape, lambda b, i: (0, 0, 0))],
        out_specs=tok(ATTN_DIM),
        out_shape=jax.ShapeDtypeStruct((bsz, sq, ATTN_DIM), BF16),
        scratch_shapes=[
            pltpu.VMEM((nt_max + nt_max % 2, tk, tq), I32),
            pltpu.VMEM((IDX_HEADS * tq, LANES), BF16),
            pltpu.VMEM((N_KV_HEADS, r4, LANES), BF16),
            pltpu.VMEM((tq, tq), BF16),
            pltpu.VMEM((SUBLANES, tq), I32),
            pltpu.VMEM((SUBLANES, LANES), F32),
            pltpu.VMEM((N_KV_HEADS, SUBLANES, r4), F32),
            pltpu.VMEM((N_KV_HEADS, SUBLANES, r4), F32),
            pltpu.VMEM((N_KV_HEADS, HEAD_DIM, r4), F32),
        ],
        compiler_params=pltpu.CompilerParams(
            dimension_semantics=("arbitrary", "arbitrary"), vmem_limit_bytes=VMEM_LIMIT),
        name="dsa",
    )(q, qi, wit, kid, kb, vb, bt, btmax)


def _t5_bucket(rel):
    nb = N_BUCKETS // 2
    max_exact = nb // 2
    ret = jnp.where(rel > 0, nb, 0)
    n = jnp.abs(rel)
    nf = jnp.maximum(n, 1).astype(F32)
    large = max_exact + (jnp.log(nf / max_exact) / math.log(MAX_DISTANCE / max_exact)
                         * (nb - max_exact)).astype(I32)
    large = jnp.minimum(large, nb - 1)
    return ret + jnp.where(n < max_exact, n, large)


def _bias_tables(rel_bias, tq, tk):
    far = rel_bias[_t5_bucket(jnp.full((1,), -(tk + 1), I32))[0]]
    period = 2 * tk + tq
    off = jnp.arange(period, dtype=I32)
    off = jnp.where(off < 2 * tk, off, off - period)
    diag = ((rel_bias[_t5_bucket(off - tk)] - far) * LOG2E).T
    tab = jnp.tile(diag, (1, tq))[:, :tq * (period - 1)].reshape(N_HEADS, tq, period - 1)[:, :, :2 * tk]
    tab = tab.reshape(N_KV_HEADS, GROUP, tq, 2, tk).transpose(3, 0, 4, 1, 2)
    tab = tab.reshape(2, N_KV_HEADS, tk, GROUP * tq).astype(F32)
    tab_max = jnp.maximum(jnp.max(tab, axis=(0, 2)), 0.0)
    tab = jnp.concatenate([jnp.zeros_like(tab[:1]), tab], axis=0)
    return tab, jnp.broadcast_to(tab_max[:, None, :], (N_KV_HEADS, SUBLANES, GROUP * tq))


def _mix_kernel(attn_ref, sga_ref, pc_ref, x_ref, wao_ref, wout_ref, g_ref, wr_ref, br_ref,
                x1_ref, h2_ref, gates_ref):
    n_sub = 2 if attn_ref.shape[0] % (2 * SUBLANES * 2) == 0 else 1
    ts = attn_ref.shape[0] // n_sub
    subs = [slice(k * ts, (k + 1) * ts) for k in range(n_sub)]
    yas = [_dot(attn_ref[r, :], wao_ref[...]) for r in subs]
    merged = [(sga_ref[r, :] * ya + pc_ref[r, :]).astype(BF16) for r, ya in zip(subs, yas)]
    mixes = [_dot(m, wout_ref[...]) for m in merged]
    x1s = [x_ref[r, :] + mix for r, mix in zip(subs, mixes)]
    hbs = [_rms(x1, g_ref[...]).astype(BF16) for x1 in x1s]
    rls = [_dot(hb, wr_ref[...]) for hb in hbs]
    for r, x1, hb, rl in zip(subs, x1s, hbs, rls):
        x1_ref[r, :] = x1
        h2_ref[r, :] = hb
        gates_ref[r, :] = _route(rl, br_ref)


def _route(rl, br_ref):
    tm = rl.shape[0]
    lane = lax.broadcasted_iota(I32, (tm, LANES), 1)
    neg_inf = F32(-jnp.inf)
    big = I32(LANES)

    def first_argmax(x):
        mx = jnp.max(x, axis=-1, keepdims=True)
        return jnp.min(jnp.where(x == mx, lane, big), axis=-1, keepdims=True)

    def pick(x, idx):
        return jnp.sum(jnp.where(lane == idx, x, 0.0), axis=-1, keepdims=True)

    glog = rl[:, :LANES]
    gvalid = lane < N_GROUPS
    g_sel = first_argmax(jnp.where(gvalid, glog + br_ref[0:1, :], neg_inf))
    gm = jnp.max(jnp.where(gvalid, glog, neg_inf), axis=-1, keepdims=True)
    gexp = jnp.where(gvalid, jnp.exp(glog - gm), 0.0)
    g_prob = pick(gexp, g_sel) / jnp.sum(gexp, axis=-1, keepdims=True)

    elog = rl[:, LANES:]
    in_grp = (lane // EXPERTS_PER_GROUP) == g_sel
    eb = jnp.where(in_grp, elog + br_ref[1:2, :], neg_inf)
    i1 = first_argmax(eb)
    i2 = first_argmax(jnp.where(lane == i1, neg_inf, eb))
    em = jnp.max(jnp.where(in_grp, elog, neg_inf), axis=-1, keepdims=True)
    eexp = jnp.where(in_grp, jnp.exp(elog - em), 0.0)
    esum = jnp.sum(eexp, axis=-1, keepdims=True)
    p1 = pick(eexp, i1) / esum
    p2 = pick(eexp, i2) / esum
    w1 = g_prob * p1 / (p1 + p2)
    w2 = g_prob * p2 / (p1 + p2)
    return jnp.where(lane == i1, w1, 0.0) + jnp.where(lane == i2, w2, 0.0)


def _mix_call(attn, sga, pc, x, wao, wout, g, wr, br, tm):
    n, d = x.shape
    tok = lambda c: pl.BlockSpec((tm, c), lambda i: (i, 0))
    full = lambda a: pl.BlockSpec(a.shape, lambda i: (0,) * a.ndim)
    return pl.pallas_call(
        _mix_kernel,
        grid=(n // tm,),
        in_specs=[tok(ATTN_DIM), tok(d), tok(d), tok(d), full(wao), full(wout), full(g), full(wr), full(br)],
        out_specs=(tok(d), tok(d), tok(LANES)),
        out_shape=(jax.ShapeDtypeStruct((n, d), F32), jax.ShapeDtypeStruct((n, d), BF16),
                   jax.ShapeDtypeStruct((n, LANES), F32)),
        compiler_params=pltpu.CompilerParams(
            dimension_semantics=("arbitrary",), vmem_limit_bytes=VMEM_LIMIT),
        name="mix",
    )(attn, sga, pc, x, wao, wout, g, wr, br)


def _moe_kernel(h_ref, gates_ref, x1_ref, wg_ref, wu_ref, wd_ref, g_ref, out_ref, acc_ref):
    e = pl.program_id(1)

    @pl.when(e == 0)
    def _():
        acc_ref[...] = jnp.zeros(acc_ref.shape, F32)

    hb = h_ref[...]
    n_e = wg_ref.shape[0]
    ups = [(_dot(hb, wg_ref[k]), _dot(hb, wu_ref[k])) for k in range(n_e)]
    acts = [(a * _sigmoid(a) * b).astype(BF16) for a, b in ups]
    ys = [_dot(act, wd_ref[k]) for k, act in enumerate(acts)]
    lane = lax.broadcasted_iota(I32, gates_ref.shape, 1)
    gates = gates_ref[...]
    acc = acc_ref[...]
    for k, y in enumerate(ys):
        ge = jnp.sum(jnp.where(lane == e * n_e + k, gates, 0.0), axis=-1, keepdims=True)
        acc = acc + ge * y
    acc_ref[...] = acc

    @pl.when(e == pl.num_programs(1) - 1)
    def _():
        out_ref[...] = _rms(x1_ref[...] + acc_ref[...], g_ref[...])


def _moe_call(hb, gates, x1, wg, wu, wd, g, tm):
    n, d = x1.shape
    ne, _, de = wg.shape
    tok = lambda c: pl.BlockSpec((tm, c), lambda i, e: (i, 0))
    per = MOE_EXPERTS_PER_STEP
    return pl.pallas_call(
        _moe_kernel,
        grid=(n // tm, ne // per),
        in_specs=[tok(d), tok(LANES), tok(d),
                  pl.BlockSpec((per, d, de), lambda i, e: (e, 0, 0)),
                  pl.BlockSpec((per, d, de), lambda i, e: (e, 0, 0)),
                  pl.BlockSpec((per, de, d), lambda i, e: (e, 0, 0)),
                  pl.BlockSpec(g.shape, lambda i, e: (0, 0))],
        out_specs=tok(d),
        out_shape=jax.ShapeDtypeStruct((n, d), F32),
        scratch_shapes=[pltpu.VMEM((tm, d), F32)],
        compiler_params=pltpu.CompilerParams(
            dimension_semantics=("arbitrary", "arbitrary"), vmem_limit_bytes=VMEM_LIMIT),
        name="moe",
    )(hb, gates, x1, wg, wu, wd, g)


def _head_pair_perm():
    cols = []
    for g in range(GROUP):
        for n in range(N_KV_HEADS):
            h = n * GROUP + g
            cols.extend(range(h * HEAD_DIM, (h + 1) * HEAD_DIM))
    return np.asarray(cols, np.int32)


def _prep_weights(w_in, w_attn_out, w_conv_out, w_out, w_group, b_group, w_router, b_router,
                  w_gate, w_up, w_down):
    d = w_in.shape[0]
    o = np.cumsum([0, ATTN_DIM, KV_DIM, KV_DIM, IDX_HEADS * IDX_DIM, IDX_DIM, IDX_HEADS,
                   CONV_DIM, CONV_DIM, CONV_DIM, d, d])
    perm = _head_pair_perm()
    wq = w_in[:, o[0]:o[1]][:, perm] * (ATTN_SCALE * LOG2E)
    wa = jnp.concatenate([wq, w_in[:, o[1]:o[3]]], axis=1).astype(BF16)
    wki = w_in[:, o[4]:o[5]]
    wwi = jnp.pad(w_in[:, o[5]:o[6]] * IDX_W_SCALE, ((0, 0), (0, LANES - IDX_HEADS)))
    wb = jnp.concatenate([w_in[:, o[3]:o[4]] * IDX_SCALE, wki, wki, wwi], axis=1).astype(BF16)
    wc = w_in[:, o[6]:o[9]].astype(BF16)
    wg = w_in[:, o[9]:o[11]].astype(BF16)
    wr = jnp.concatenate([jnp.pad(w_group, ((0, 0), (0, LANES - N_GROUPS))),
                          jnp.pad(w_router, ((0, 0), (0, LANES - N_EXPERTS)))], axis=1).astype(BF16)
    br = jnp.stack([jnp.pad(b_group, (0, LANES - N_GROUPS)),
                    jnp.pad(b_router, (0, LANES - N_EXPERTS))]).astype(F32)
    return dict(wa=wa, wb=wb, wc=wc, wg=wg, wao=w_attn_out[perm, :].astype(BF16),
                wco=w_conv_out.astype(BF16), wout=w_out.astype(BF16), wr=wr, br=br,
                wgate=w_gate.astype(BF16), wup=w_up.astype(BF16), wdown=w_down.astype(BF16))


def _tile(n, pref):
    t = min(n, pref)
    assert n % t == 0, (n, pref)
    return t


def _layer(x, past, w, norm_mix, w_conv, norm_ffn, rel_bias):
    bsz, t, d = x.shape
    tk = KEY_TILE
    cinit = jnp.zeros((bsz, CONV_W - 1, CONV_DIM), F32) if past is None else past[3].astype(F32)
    (q, k, v, kb, vb, qi, kidx, kid, wi, sga, pc, cst) = _proj_call(
        x, cinit, norm_mix[None, :], w["wa"], w["wb"], w["wc"], w["wg"], w_conv, w["wco"], _tile(t, 512))

    if past is None:
        assert t % tk == 0
        tq, q_off, total = tk, 0, t
        n_valid_last = tk
        kid_all, kb_all, vb_all = kid, kb, vb
    else:
        p_len = past[0].shape[1]
        tq = ROW_STRIP
        assert p_len % tk == 0 and t <= tq
        q_off, total = p_len, p_len + t
        n_valid_last = t
        qpad = ((0, 0), (0, tq - t), (0, 0))
        q, qi, wi = jnp.pad(q, qpad), jnp.pad(qi, qpad), jnp.pad(wi, qpad)
        pad = ((0, 0), (0, tk - t), (0, 0))
        pk = past[0].reshape(bsz, p_len, KV_DIM).astype(BF16)
        pv = past[1].reshape(bsz, p_len, KV_DIM).astype(BF16)
        pki = past[2].astype(BF16)
        kb_all = jnp.concatenate([pk, jnp.pad(kb, pad)], axis=1)
        vb_all = jnp.concatenate([pv, jnp.pad(vb, pad)], axis=1)
        kid_all = jnp.concatenate([jnp.concatenate([pki, pki], axis=-1), jnp.pad(kid, pad)], axis=1)
    nt = kb_all.shape[1] // tk
    tiles = lambda a: a.reshape(bsz, nt, tk, LANES)
    wit = jnp.swapaxes(wi[:, :, :SUBLANES], 1, 2)
    vbt = jnp.swapaxes(tiles(vb_all), 2, 3)
    bt, btmax = _bias_tables(rel_bias, tq, tk)
    attn = _dsa_call(q, qi, wit, tiles(kid_all), tiles(kb_all), vbt, bt, btmax, tq=tq, q_off=q_off,
                     n_valid_last=n_valid_last, topk=min(TOPK_MAX, total // 4))
    attn = attn[:, :t]

    n = bsz * t
    tm = _tile(n, 512)
    x1, h2, gates = _mix_call(attn.reshape(n, ATTN_DIM), sga.reshape(n, d), pc.reshape(n, d),
                              x.reshape(n, d), w["wao"], w["wout"], norm_ffn[None, :], w["wr"], w["br"], tm)
    state = (k.reshape(bsz, t, N_KV_HEADS, HEAD_DIM), v.reshape(bsz, t, N_KV_HEADS, HEAD_DIM), kidx, cst)
    return x1, h2, gates, state


def kernel(x_prompt, x_sample, cache_k, cache_v, cache_kidx, state_conv, rel_bias, norm_mix, w_in,
           w_attn_out, w_conv, w_conv_out, w_out, norm_ffn, w_group, b_group, w_router, b_router,
           w_gate, w_up, w_down, norm_final):
    depth = w_in.shape[0]
    assert depth == 1, "the final norm is fused into the last layer's expert kernel"
    l = 0
    w = _prep_weights(w_in[l], w_attn_out[l], w_conv_out[l], w_out[l], w_group[l], b_group[l],
                      w_router[l], b_router[l], w_gate[l], w_up[l], w_down[l])
    outs = []
    states = []
    for x, past in ((x_prompt, None),
                    (x_sample, (cache_k[l], cache_v[l], cache_kidx[l], state_conv[l]))):
        bsz, t, d = x.shape
        x1, h2, gates, state = _layer(x, past, w, norm_mix[l], w_conv[l], norm_ffn[l], rel_bias)
        n = bsz * t
        y = _moe_call(h2, gates, x1, w["wgate"], w["wup"], w["wdown"], norm_final[None, :], _tile(n, 512))
        outs.append(y.reshape(bsz, t, d))
        states.append(state)
    sp, ss = states
    return (outs[0], outs[1], sp[0][None], sp[1][None], sp[2][None], sp[3][None],
            ss[0][None], ss[1][None], ss[2][None], ss[3][None])
```

```python
import functools
import math

import jax
import jax.numpy as jnp
import numpy as np
from jax import lax
from jax.experimental import pallas as pl
from jax.experimental.pallas import tpu as pltpu

F32 = jnp.float32
BF16 = jnp.bfloat16
I32 = jnp.int32

CHUNK = 64
N_HEADS = 8
N_KV_HEADS = 2
GROUP = N_HEADS // N_KV_HEADS
HEAD_DIM = 64
ATTN_DIM = N_HEADS * HEAD_DIM
KV_DIM = N_KV_HEADS * HEAD_DIM
IDX_HEADS = 4
IDX_DIM = 64
TOPK_MAX = 256
CONV_DIM = 512
CONV_W = 3
N_BUCKETS = 32
MAX_DISTANCE = 128
N_GROUPS = 4
EXPERTS_PER_GROUP = 4
N_EXPERTS = N_GROUPS * EXPERTS_PER_GROUP
EPS = 1e-6
ATTN_SCALE = HEAD_DIM ** -0.5
IDX_SCALE = IDX_DIM ** -0.5
IDX_W_SCALE = IDX_HEADS ** -0.5

LANES = 128
SUBLANES = 8
KEY_TILE = 256
ROW_STRIP = 128
MOE_EXPERTS_PER_STEP = 4
VMEM_LIMIT = 56 * 1024 * 1024
INT_MIN = -2 ** 31
IDX_BIG = 2 ** 30
F32_LOWEST = float(np.finfo(np.float32).min)
LOG2E = math.log2(math.e)
BOUND_SLACK = 1.0 + 2.0 ** -5
L_UNDERFLOW = 2.0 ** -80


def _dot(a, b):
    return jnp.dot(a, b, preferred_element_type=F32)


def _dot_nt(a, b):
    return lax.dot_general(a, b, (((1,), (1,)), ((), ())), preferred_element_type=F32)


def _sigmoid(x):
    return 1.0 / (1.0 + jnp.exp(-x))


def _rms(x, g):
    return x * lax.rsqrt(jnp.mean(x * x, axis=-1, keepdims=True) + EPS) * g


def _proj_kernel(x_ref, cinit_ref, g_ref, wa_ref, wb_ref, wc_ref, wg_ref, wconv_ref, wco_ref,
                 q_ref, k_ref, v_ref, kb_ref, vb_ref, qi_ref, kidx_ref, kid_ref, wi_ref,
                 sga_ref, pc_ref, cst_ref, carry_ref, *, tm):
    j = pl.program_id(1)
    hb = _rms(x_ref[0], g_ref[...]).astype(BF16)

    a = _dot(hb, wa_ref[...])
    q_ref[0] = a[:, :ATTN_DIM].astype(BF16)
    k = a[:, ATTN_DIM:ATTN_DIM + KV_DIM]
    v = a[:, ATTN_DIM + KV_DIM:]
    k_ref[0] = k
    v_ref[0] = v
    kb_ref[0] = k.astype(BF16)
    vb_ref[0] = v.astype(BF16)

    b = _dot(hb, wb_ref[...])
    nqi = IDX_HEADS * IDX_DIM
    qi_ref[0] = b[:, :nqi].astype(BF16)
    kid_ref[0] = b[:, nqi:nqi + LANES].astype(BF16)
    kidx_ref[0] = b[:, nqi:nqi + IDX_DIM]
    wi_ref[0] = b[:, nqi + LANES:]

    c = _dot(hb, wc_ref[...])
    cb = c[:, :CONV_DIM]
    u = c[:, CONV_DIM:2 * CONV_DIM] * c[:, 2 * CONV_DIM:]

    @pl.when(j == 0)
    def _():
        carry_ref[6:8, :] = cinit_ref[0]

    c2 = carry_ref[6:7, :]
    c1 = carry_ref[7:8, :]
    row = lax.broadcasted_iota(I32, (tm, CONV_DIM), 0)
    up1 = jnp.where(row == 0, c1, pltpu.roll(u, 1, axis=0))
    up2 = jnp.where(row == 0, c2, jnp.where(row == 1, c1, pltpu.roll(u, 2, axis=0)))
    conv = wconv_ref[0:1, :] * up2 + wconv_ref[1:2, :] * up1 + wconv_ref[2:3, :] * u
    tail = u[tm - 2:tm, :]
    carry_ref[6:8, :] = tail
    cst_ref[0] = tail
    yc = _dot((cb * conv).astype(BF16), wco_ref[...])

    gt = _dot(hb, wg_ref[...])
    d = yc.shape[-1]
    sga_ref[0] = _sigmoid(gt[:, :d])
    pc_ref[0] = _sigmoid(gt[:, d:]) * yc


def _proj_call(x, cinit, g, wa, wb, wc, wg, wconv, wco, tm):
    bsz, s, d = x.shape
    grid = (bsz, s // tm)
    tok = lambda n: pl.BlockSpec((1, tm, n), lambda b, j: (b, j, 0))
    full = lambda a: pl.BlockSpec(a.shape, lambda b, j: (0,) * a.ndim)
    st = pl.BlockSpec((1, CONV_W - 1, CONV_DIM), lambda b, j: (b, 0, 0))
    out_shape = (
        jax.ShapeDtypeStruct((bsz, s, ATTN_DIM), BF16),
        jax.ShapeDtypeStruct((bsz, s, KV_DIM), F32),
        jax.ShapeDtypeStruct((bsz, s, KV_DIM), F32),
        jax.ShapeDtypeStruct((bsz, s, KV_DIM), BF16),
        jax.ShapeDtypeStruct((bsz, s, KV_DIM), BF16),
        jax.ShapeDtypeStruct((bsz, s, IDX_HEADS * IDX_DIM), BF16),
        jax.ShapeDtypeStruct((bsz, s, IDX_DIM), F32),
        jax.ShapeDtypeStruct((bsz, s, LANES), BF16),
        jax.ShapeDtypeStruct((bsz, s, LANES), F32),
        jax.ShapeDtypeStruct((bsz, s, d), F32),
        jax.ShapeDtypeStruct((bsz, s, d), F32),
        jax.ShapeDtypeStruct((bsz, CONV_W - 1, CONV_DIM), F32),
    )
    out_specs = (tok(ATTN_DIM), tok(KV_DIM), tok(KV_DIM), tok(KV_DIM), tok(KV_DIM),
                 tok(IDX_HEADS * IDX_DIM), tok(IDX_DIM), tok(LANES), tok(LANES), tok(d), tok(d), st)
    return pl.pallas_call(
        functools.partial(_proj_kernel, tm=tm),
        grid=grid,
        in_specs=[tok(d), st, full(g), full(wa), full(wb), full(wc), full(wg), full(wconv), full(wco)],
        out_specs=out_specs,
        out_shape=out_shape,
        scratch_shapes=[pltpu.VMEM((8, CONV_DIM), F32)],
        compiler_params=pltpu.CompilerParams(
            dimension_semantics=("arbitrary", "arbitrary"), vmem_limit_bytes=VMEM_LIMIT),
        name="proj",
    )(x, cinit, g, wa, wb, wc, wg, wconv, wco)


def _tree_sum(xs):
    while len(xs) > 1:
        xs = [xs[a] + xs[a + 1] for a in range(0, len(xs) - 1, 2)] + ([xs[-1]] if len(xs) % 2 else [])
    return xs[0]


def _dsa_kernel(qt_ref, qit_ref, wit_ref, kid_ref, kb_ref, vbt_ref, bt_ref, btmax_ref, out_ref,
                keys_ref, qist_ref, qst_ref, eye_ref, jcut_ref, kmax_ref, m_ref, l_ref, acc_ref,
                *, tq, tk, q_off, n_valid_last, topk, idx_bits):
    i = pl.program_id(1)
    nt = (q_off + i * tq) // tk + 1
    neg_inf = F32(-jnp.inf)

    row = lax.broadcasted_iota(I32, (LANES, tq), 0)
    top = row < HEAD_DIM
    zero_b = jnp.zeros((LANES, tq), BF16)
    for h in range(IDX_HEADS):
        blk = qit_ref[0, (h // 2) * LANES:(h // 2 + 1) * LANES, :]
        qist_ref[:, h * tq:(h + 1) * tq] = jnp.where(top if h % 2 == 0 else ~top, blk, zero_b)
    for n in range(N_KV_HEADS):
        for g in range(GROUP):
            blk = qt_ref[0, g * LANES:(g + 1) * LANES, :]
            qst_ref[n, :, g * tq:(g + 1) * tq] = jnp.where(top if n == 0 else ~top, blk, zero_b)
    eye_ref[...] = jnp.where(lax.broadcasted_iota(I32, (tq, tq), 0) == lax.broadcasted_iota(I32, (tq, tq), 1),
                             1.0, 0.0).astype(BF16)
    w_row = jnp.concatenate([wit_ref[0, h:h + 1, :] for h in range(IDX_HEADS)], axis=1)

    krow = lax.broadcasted_iota(I32, (tk, tq), 0)
    qcol = lax.broadcasted_iota(I32, (tk, tq), 1)
    lim_last = jnp.minimum(((qcol >> 6) + 1) << 6, n_valid_last)

    def score_tiles(ts):
        ss = [_dot(kid_ref[0, t], qist_ref[...]) for t in ts]
        for t, s in zip(ts, ss):
            s = jnp.maximum(s, 0.0) * w_row
            sc = (s[:, 0:tq] + s[:, tq:2 * tq]) + (s[:, 2 * tq:3 * tq] + s[:, 3 * tq:4 * tq])
            sc = jnp.where(sc == 0.0, 0.0, sc)
            bits = lax.bitcast_convert_type(sc, I32)
            key = bits ^ ((bits >> 31) & 0x7FFFFFFF)
            lim = jnp.where(t == nt - 1, lim_last, tk)
            keys_ref[t] = jnp.where(krow < lim, key, INT_MIN)

    def score_body(j, carry):
        score_tiles([2 * j, 2 * j + 1])
        return carry

    lax.fori_loop(0, nt // 2, score_body, 0)

    @pl.when(nt % 2 == 1)
    def _():
        score_tiles([nt - 1])

    @pl.when(nt % 2 == 1)
    def _():
        keys_ref[nt] = jnp.full((tk, tq), INT_MIN, I32)

    def count(ind_fn, cand):
        cand8 = jnp.broadcast_to(cand, (SUBLANES, tq))
        row8 = lax.broadcasted_iota(I32, (SUBLANES, tq), 0)

        def body(j, acc):
            for t in (2 * j, 2 * j + 1):
                parts = []
                for r in range(tk // SUBLANES):
                    x = keys_ref[t, r * SUBLANES:(r + 1) * SUBLANES, :]
                    parts.append(ind_fn(x, row8 + (t * tk + r * SUBLANES), cand8))
                acc = acc + _tree_sum(parts)
            return acc

        acc = lax.fori_loop(0, (nt + 1) // 2, body, jnp.zeros((SUBLANES, tq), I32))
        return jnp.sum(acc.astype(F32), axis=0, keepdims=True)

    ge = lambda x, idx, c: jnp.where(x >= c, 1, 0)

    def bit_body(b, thr):
        cand = thr + (jnp.int32(1) << (31 - b))
        return jnp.where(count(ge, cand) >= topk, cand, thr)

    thr = lax.fori_loop(0, 32, bit_body, jnp.full((1, tq), INT_MIN, I32))
    thr8 = jnp.broadcast_to(thr, (SUBLANES, tq))

    jcut_ref[...] = jnp.full(jcut_ref.shape, IDX_BIG, I32)
    n_ge = count(ge, thr)

    @pl.when(jnp.max(n_ge) > topk)
    def _():
        need = topk - count(ge, thr + 1)
        tied_lt = lambda x, idx, c: jnp.where(x == thr8, jnp.where(idx < c, 1, 0), 0)

        def jbit_body(b, jj):
            cand = jj + (jnp.int32(1) << (idx_bits - 1 - b))
            return jnp.where(count(tied_lt, cand) <= need, cand, jj)

        jj = lax.fori_loop(0, idx_bits, jbit_body, jnp.zeros((1, tq), I32))
        jcut_ref[...] = jnp.broadcast_to(jj, jcut_ref.shape)

    jcut = jnp.where(thr == INT_MIN, 0, jcut_ref[0:1, :])

    @pl.when(i == 0)
    def _():
        half0 = lax.broadcasted_iota(I32, (tk, LANES), 1) < HEAD_DIM

        def kbody(t, c):
            k = kb_ref[0, t].astype(F32)
            sq = k * k
            s0 = jnp.max(jnp.sum(jnp.where(half0, sq, 0.0), axis=1, keepdims=True), axis=0, keepdims=True)
            s1 = jnp.max(jnp.sum(jnp.where(half0, 0.0, sq), axis=1, keepdims=True), axis=0, keepdims=True)
            return jnp.maximum(c[0], s0), jnp.maximum(c[1], s1)

        z = jnp.zeros((1, 1), F32)
        c = lax.fori_loop(0, kb_ref.shape[1], kbody, (z, z))
        for n in range(N_KV_HEADS):
            kmax_ref[n:n + 1, :] = jnp.broadcast_to(jnp.sqrt(c[n]), (1, LANES))

    def fold8(x, op):
        parts = [x[r * SUBLANES:(r + 1) * SUBLANES, :] for r in range(x.shape[0] // SUBLANES)]
        while len(parts) > 1:
            parts = [op(parts[a], parts[a + 1]) for a in range(0, len(parts), 2)]
        return parts[0]

    mb = []
    for n in range(N_KV_HEADS):
        qf = qst_ref[n].astype(F32)
        qsq = jnp.sum(fold8(qf * qf, jnp.add), axis=0, keepdims=True)
        mb.append(BOUND_SLACK * jnp.sqrt(qsq) * kmax_ref[n:n + 1, 0:1] + btmax_ref[n, 0:1, :])

    def logits(t, near, n, sel4):
        lg = _dot(kb_ref[0, t], qst_ref[n]) + sel4
        if near is not None:
            lg = lg + bt_ref[near, n]
        return lg

    def selection(t):
        key = keys_ref[t]
        sel = jnp.where(key > thr, 0.0,
                        jnp.where(key == thr, jnp.where(krow + t * tk < jcut, 0.0, neg_inf), neg_inf))
        return jnp.concatenate([sel] * GROUP, axis=1)

    def tiles_fast(specs):
        sel4 = [selection(t) for t, _ in specs]
        units = [(t, near, n) for t, near in specs for n in range(N_KV_HEADS)]
        lgs = [logits(t, near, n, sel4[u // N_KV_HEADS]) for u, (t, near, n) in enumerate(units)]
        ps = [jnp.exp2(lg - mb[n]) for lg, (_, _, n) in zip(lgs, units)]
        for n in range(N_KV_HEADS):
            l_ref[n] = l_ref[n] + _tree_sum([fold8(p, jnp.add) for p, u in zip(ps, units) if u[2] == n])
        pvs = [_dot(vbt_ref[0, t, n * HEAD_DIM:(n + 1) * HEAD_DIM, :], p.astype(BF16))
               for p, (t, _, n) in zip(ps, units)]
        for n in range(N_KV_HEADS):
            acc_ref[n] = acc_ref[n] + _tree_sum([pv for pv, u in zip(pvs, units) if u[2] == n])

    def tiles_exact(specs):
        for t, near in specs:
            tile_exact(t, near)

    def tile_exact(t, near):
        sel4 = selection(t)
        vt = vbt_ref[0, t]
        for n in range(N_KV_HEADS):
            lg = logits(t, near, n, sel4)
            m_old = m_ref[n, 0:1, :]
            m_new = jnp.maximum(m_old, jnp.max(fold8(lg, jnp.maximum), axis=0, keepdims=True))
            m_safe = jnp.maximum(m_new, F32_LOWEST)
            alpha = jnp.exp2(m_old - m_safe)
            p = jnp.exp2(lg - m_safe)
            m_ref[n] = jnp.broadcast_to(m_new, m_ref.shape[1:])
            l_ref[n] = alpha * l_ref[n] + fold8(p, jnp.add)
            pv = _dot(vt[n * HEAD_DIM:(n + 1) * HEAD_DIM, :], p.astype(BF16))
            acc_ref[n] = alpha * acc_ref[n] + pv

    def run_tiles(tiles, unroll):
        l_ref[...] = jnp.zeros(l_ref.shape, F32)
        acc_ref[...] = jnp.zeros(acc_ref.shape, F32)
        n_far = jnp.maximum(nt - 2, 0)

        def far_body(j, carry):
            tiles([(j * unroll + u, None) for u in range(unroll)])
            return carry

        lax.fori_loop(0, n_far // unroll, far_body, 0)
        for u in range(unroll - 1):
            @pl.when(n_far % unroll > u)
            def _():
                tiles([((n_far // unroll) * unroll + u, None)])

        @pl.when(nt >= 2)
        def _():
            tiles([(nt - 2, 0), (nt - 1, 1)])

        @pl.when(nt < 2)
        def _():
            tiles([(nt - 1, 1)])

    run_tiles(tiles_fast, 2)
    l_min = jnp.min(jnp.minimum(jnp.sum(l_ref[0], axis=0, keepdims=True),
                                jnp.sum(l_ref[1], axis=0, keepdims=True)))

    @pl.when(jnp.logical_not(l_min >= L_UNDERFLOW))
    def _():
        m_ref[...] = jnp.full(m_ref.shape, neg_inf, F32)
        run_tiles(tiles_exact, 1)

    o = [acc_ref[n] / jnp.sum(l_ref[n], axis=0, keepdims=True) for n in range(N_KV_HEADS)]
    for g in range(GROUP):
        pair = jnp.concatenate([o[n][:, g * tq:(g + 1) * tq] for n in range(N_KV_HEADS)], axis=0)
        out_ref[0, :, g * LANES:(g + 1) * LANES] = _dot_nt(eye_ref[...], pair.astype(BF16)).astype(BF16)


def _dsa_call(qt, qit, wit, kid, kb, vb, bt, btmax, *, tq, q_off, n_valid_last, topk):
    bsz, _, sq = qt.shape
    nt_max, tk = kid.shape[1], kid.shape[2]
    idx_bits = int(nt_max * tk).bit_length()
    r4 = GROUP * tq
    tok = lambda n: pl.BlockSpec((1, tq, n), lambda b, i: (b, i, 0))
    tokt = lambda n: pl.BlockSpec((1, n, tq), lambda b, i: (b, 0, i))
    keys = pl.BlockSpec((1, nt_max, tk, LANES), lambda b, i: (b, 0, 0, 0))
    kern = functools.partial(_dsa_kernel, tq=tq, tk=tk, q_off=q_off, n_valid_last=n_valid_last,
                             topk=topk, idx_bits=idx_bits)
    return pl.pallas_call(
        kern,
        grid=(bsz, sq // tq),
        in_specs=[tokt(ATTN_DIM), tokt(IDX_HEADS * IDX_DIM), tokt(SUBLANES), keys, keys,
                  pl.BlockSpec((1, nt_max, KV_DIM, tk), lambda b, i: (b, 0, 0, 0)),
                  pl.BlockSpec(bt.shape, lambda b, i: (0, 0, 0, 0)),
                  pl.BlockSpec(btmax.shape, lambda b, i: (0, 0, 0))],
        out_specs=tok(ATTN_DIM),
        out_shape=jax.ShapeDtypeStruct((bsz, sq, ATTN_DIM), BF16),
        scratch_shapes=[
            pltpu.VMEM((nt_max + nt_max % 2, tk, tq), I32),
            pltpu.VMEM((LANES, IDX_HEADS * tq), BF16),
            pltpu.VMEM((N_KV_HEADS, LANES, r4), BF16),
            pltpu.VMEM((tq, tq), BF16),
            pltpu.VMEM((SUBLANES, tq), I32),
            pltpu.VMEM((SUBLANES, LANES), F32),
            pltpu.VMEM((N_KV_HEADS, SUBLANES, r4), F32),
            pltpu.VMEM((N_KV_HEADS, SUBLANES, r4), F32),
            pltpu.VMEM((N_KV_HEADS, HEAD_DIM, r4), F32),
        ],
        compiler_params=pltpu.CompilerParams(
            dimension_semantics=("arbitrary", "arbitrary"), vmem_limit_bytes=VMEM_LIMIT),
        name="dsa",
    )(qt, qit, wit, kid, kb, vb, bt, btmax)


def _t5_bucket(rel):
    nb = N_BUCKETS // 2
    max_exact = nb // 2
    ret = jnp.where(rel > 0, nb, 0)
    n = jnp.abs(rel)
    nf = jnp.maximum(n, 1).astype(F32)
    large = max_exact + (jnp.log(nf / max_exact) / math.log(MAX_DISTANCE / max_exact)
                         * (nb - max_exact)).astype(I32)
    large = jnp.minimum(large, nb - 1)
    return ret + jnp.where(n < max_exact, n, large)


def _bias_tables(rel_bias, tq, tk):
    far = rel_bias[_t5_bucket(jnp.full((1,), -(tk + 1), I32))[0]]
    period = 2 * tk + tq
    off = jnp.arange(period, dtype=I32)
    off = jnp.where(off < 2 * tk, off, off - period)
    diag = ((rel_bias[_t5_bucket(off - tk)] - far) * LOG2E).T
    tab = jnp.tile(diag, (1, tq))[:, :tq * (period - 1)].reshape(N_HEADS, tq, period - 1)[:, :, :2 * tk]
    tab = tab.reshape(N_KV_HEADS, GROUP, tq, 2, tk).transpose(3, 0, 4, 1, 2)
    tab = tab.reshape(2, N_KV_HEADS, tk, GROUP * tq).astype(F32)
    tab_max = jnp.maximum(jnp.max(tab, axis=(0, 2)), 0.0)
    return tab, jnp.broadcast_to(tab_max[:, None, :], (N_KV_HEADS, SUBLANES, GROUP * tq))


def _mix_kernel(attn_ref, sga_ref, pc_ref, x_ref, wao_ref, wout_ref, g_ref, wr_ref, br_ref,
                x1_ref, h2_ref, gates_ref):
    n_sub = 2 if attn_ref.shape[0] % (2 * SUBLANES * 2) == 0 else 1
    ts = attn_ref.shape[0] // n_sub
    subs = [slice(k * ts, (k + 1) * ts) for k in range(n_sub)]
    yas = [_dot(attn_ref[r, :], wao_ref[...]) for r in subs]
    merged = [(sga_ref[r, :] * ya + pc_ref[r, :]).astype(BF16) for r, ya in zip(subs, yas)]
    mixes = [_dot(m, wout_ref[...]) for m in merged]
    x1s = [x_ref[r, :] + mix for r, mix in zip(subs, mixes)]
    hbs = [_rms(x1, g_ref[...]).astype(BF16) for x1 in x1s]
    rls = [_dot(hb, wr_ref[...]) for hb in hbs]
    for r, x1, hb, rl in zip(subs, x1s, hbs, rls):
        x1_ref[r, :] = x1
        h2_ref[r, :] = hb
        gates_ref[r, :] = _route(rl, br_ref)


def _route(rl, br_ref):
    tm = rl.shape[0]
    lane = lax.broadcasted_iota(I32, (tm, LANES), 1)
    neg_inf = F32(-jnp.inf)
    big = I32(LANES)

    def first_argmax(x):
        mx = jnp.max(x, axis=-1, keepdims=True)
        return jnp.min(jnp.where(x == mx, lane, big), axis=-1, keepdims=True)

    def pick(x, idx):
        return jnp.sum(jnp.where(lane == idx, x, 0.0), axis=-1, keepdims=True)

    glog = rl[:, :LANES]
    gvalid = lane < N_GROUPS
    g_sel = first_argmax(jnp.where(gvalid, glog + br_ref[0:1, :], neg_inf))
    gm = jnp.max(jnp.where(gvalid, glog, neg_inf), axis=-1, keepdims=True)
    gexp = jnp.where(gvalid, jnp.exp(glog - gm), 0.0)
    g_prob = pick(gexp, g_sel) / jnp.sum(gexp, axis=-1, keepdims=True)

    elog = rl[:, LANES:]
    in_grp = (lane // EXPERTS_PER_GROUP) == g_sel
    eb = jnp.where(in_grp, elog + br_ref[1:2, :], neg_inf)
    i1 = first_argmax(eb)
    i2 = first_argmax(jnp.where(lane == i1, neg_inf, eb))
    em = jnp.max(jnp.where(in_grp, elog, neg_inf), axis=-1, keepdims=True)
    eexp = jnp.where(in_grp, jnp.exp(elog - em), 0.0)
    esum = jnp.sum(eexp, axis=-1, keepdims=True)
    p1 = pick(eexp, i1) / esum
    p2 = pick(eexp, i2) / esum
    w1 = g_prob * p1 / (p1 + p2)
    w2 = g_prob * p2 / (p1 + p2)
    return jnp.where(lane == i1, w1, 0.0) + jnp.where(lane == i2, w2, 0.0)


def _mix_call(attn, sga, pc, x, wao, wout, g, wr, br, tm):
    n, d = x.shape
    tok = lambda c: pl.BlockSpec((tm, c), lambda i: (i, 0))
    full = lambda a: pl.BlockSpec(a.shape, lambda i: (0,) * a.ndim)
    return pl.pallas_call(
        _mix_kernel,
        grid=(n // tm,),
        in_specs=[tok(ATTN_DIM), tok(d), tok(d), tok(d), full(wao), full(wout), full(g), full(wr), full(br)],
        out_specs=(tok(d), tok(d), tok(LANES)),
        out_shape=(jax.ShapeDtypeStruct((n, d), F32), jax.ShapeDtypeStruct((n, d), BF16),
                   jax.ShapeDtypeStruct((n, LANES), F32)),
        compiler_params=pltpu.CompilerParams(
            dimension_semantics=("arbitrary",), vmem_limit_bytes=VMEM_LIMIT),
        name="mix",
    )(attn, sga, pc, x, wao, wout, g, wr, br)


def _moe_kernel(h_ref, gates_ref, x1_ref, wg_ref, wu_ref, wd_ref, g_ref, out_ref, acc_ref):
    e = pl.program_id(1)

    @pl.when(e == 0)
    def _():
        acc_ref[...] = jnp.zeros(acc_ref.shape, F32)

    hb = h_ref[...]
    n_e = wg_ref.shape[0]
    ups = [(_dot(hb, wg_ref[k]), _dot(hb, wu_ref[k])) for k in range(n_e)]
    acts = [(a * _sigmoid(a) * b).astype(BF16) for a, b in ups]
    ys = [_dot(act, wd_ref[k]) for k, act in enumerate(acts)]
    lane = lax.broadcasted_iota(I32, gates_ref.shape, 1)
    gates = gates_ref[...]
    acc = acc_ref[...]
    for k, y in enumerate(ys):
        ge = jnp.sum(jnp.where(lane == e * n_e + k, gates, 0.0), axis=-1, keepdims=True)
        acc = acc + ge * y
    acc_ref[...] = acc

    @pl.when(e == pl.num_programs(1) - 1)
    def _():
        out_ref[...] = _rms(x1_ref[...] + acc_ref[...], g_ref[...])


def _moe_call(hb, gates, x1, wg, wu, wd, g, tm):
    n, d = x1.shape
    ne, _, de = wg.shape
    tok = lambda c: pl.BlockSpec((tm, c), lambda i, e: (i, 0))
    per = MOE_EXPERTS_PER_STEP
    return pl.pallas_call(
        _moe_kernel,
        grid=(n // tm, ne // per),
        in_specs=[tok(d), tok(LANES), tok(d),
                  pl.BlockSpec((per, d, de), lambda i, e: (e, 0, 0)),
                  pl.BlockSpec((per, d, de), lambda i, e: (e, 0, 0)),
                  pl.BlockSpec((per, de, d), lambda i, e: (e, 0, 0)),
                  pl.BlockSpec(g.shape, lambda i, e: (0, 0))],
        out_specs=tok(d),
        out_shape=jax.ShapeDtypeStruct((n, d), F32),
        scratch_shapes=[pltpu.VMEM((tm, d), F32)],
        compiler_params=pltpu.CompilerParams(
            dimension_semantics=("arbitrary", "arbitrary"), vmem_limit_bytes=VMEM_LIMIT),
        name="moe",
    )(hb, gates, x1, wg, wu, wd, g)


def _head_pair_perm():
    cols = []
    for g in range(GROUP):
        for n in range(N_KV_HEADS):
            h = n * GROUP + g
            cols.extend(range(h * HEAD_DIM, (h + 1) * HEAD_DIM))
    return np.asarray(cols, np.int32)


def _prep_weights(w_in, w_attn_out, w_conv_out, w_out, w_group, b_group, w_router, b_router,
                  w_gate, w_up, w_down):
    d = w_in.shape[0]
    o = np.cumsum([0, ATTN_DIM, KV_DIM, KV_DIM, IDX_HEADS * IDX_DIM, IDX_DIM, IDX_HEADS,
                   CONV_DIM, CONV_DIM, CONV_DIM, d, d])
    perm = _head_pair_perm()
    wq = w_in[:, o[0]:o[1]][:, perm] * (ATTN_SCALE * LOG2E)
    wa = jnp.concatenate([wq, w_in[:, o[1]:o[3]]], axis=1).astype(BF16)
    wki = w_in[:, o[4]:o[5]]
    wwi = jnp.pad(w_in[:, o[5]:o[6]] * IDX_W_SCALE, ((0, 0), (0, LANES - IDX_HEADS)))
    wb = jnp.concatenate([w_in[:, o[3]:o[4]] * IDX_SCALE, wki, wki, wwi], axis=1).astype(BF16)
    wc = w_in[:, o[6]:o[9]].astype(BF16)
    wg = w_in[:, o[9]:o[11]].astype(BF16)
    wr = jnp.concatenate([jnp.pad(w_group, ((0, 0), (0, LANES - N_GROUPS))),
                          jnp.pad(w_router, ((0, 0), (0, LANES - N_EXPERTS)))], axis=1).astype(BF16)
    br = jnp.stack([jnp.pad(b_group, (0, LANES - N_GROUPS)),
                    jnp.pad(b_router, (0, LANES - N_EXPERTS))]).astype(F32)
    return dict(wa=wa, wb=wb, wc=wc, wg=wg, wao=w_attn_out[perm, :].astype(BF16),
                wco=w_conv_out.astype(BF16), wout=w_out.astype(BF16), wr=wr, br=br,
                wgate=w_gate.astype(BF16), wup=w_up.astype(BF16), wdown=w_down.astype(BF16))


def _tile(n, pref):
    t = min(n, pref)
    assert n % t == 0, (n, pref)
    return t


def _layer(x, past, w, norm_mix, w_conv, norm_ffn, rel_bias):
    bsz, t, d = x.shape
    tk = KEY_TILE
    cinit = jnp.zeros((bsz, CONV_W - 1, CONV_DIM), F32) if past is None else past[3].astype(F32)
    (q, k, v, kb, vb, qi, kidx, kid, wi, sga, pc, cst) = _proj_call(
        x, cinit, norm_mix[None, :], w["wa"], w["wb"], w["wc"], w["wg"], w_conv, w["wco"], _tile(t, 512))

    if past is None:
        assert t % tk == 0
        tq, q_off, total = tk, 0, t
        n_valid_last = tk
        kid_all, kb_all, vb_all = kid, kb, vb
    else:
        p_len = past[0].shape[1]
        tq = ROW_STRIP
        assert p_len % tk == 0 and t <= tq
        q_off, total = p_len, p_len + t
        n_valid_last = t
        qpad = ((0, 0), (0, tq - t), (0, 0))
        q, qi, wi = jnp.pad(q, qpad), jnp.pad(qi, qpad), jnp.pad(wi, qpad)
        pad = ((0, 0), (0, tk - t), (0, 0))
        pk = past[0].reshape(bsz, p_len, KV_DIM).astype(BF16)
        pv = past[1].reshape(bsz, p_len, KV_DIM).astype(BF16)
        pki = past[2].astype(BF16)
        kb_all = jnp.concatenate([pk, jnp.pad(kb, pad)], axis=1)
        vb_all = jnp.concatenate([pv, jnp.pad(vb, pad)], axis=1)
        kid_all = jnp.concatenate([jnp.concatenate([pki, pki], axis=-1), jnp.pad(kid, pad)], axis=1)
    nt = kb_all.shape[1] // tk
    tiles = lambda a: a.reshape(bsz, nt, tk, LANES)
    wit = jnp.swapaxes(wi[:, :, :SUBLANES], 1, 2)
    vbt = jnp.swapaxes(tiles(vb_all), 2, 3)
    bt, btmax = _bias_tables(rel_bias, tq, tk)
    attn = _dsa_call(jnp.swapaxes(q, 1, 2), jnp.swapaxes(qi, 1, 2), wit,
                     tiles(kid_all), tiles(kb_all), vbt, bt, btmax, tq=tq, q_off=q_off,
                     n_valid_last=n_valid_last, topk=min(TOPK_MAX, total // 4))
    attn = attn[:, :t]

    n = bsz * t
    tm = _tile(n, 512)
    x1, h2, gates = _mix_call(attn.reshape(n, ATTN_DIM), sga.reshape(n, d), pc.reshape(n, d),
                              x.reshape(n, d), w["wao"], w["wout"], norm_ffn[None, :], w["wr"], w["br"], tm)
    state = (k.reshape(bsz, t, N_KV_HEADS, HEAD_DIM), v.reshape(bsz, t, N_KV_HEADS, HEAD_DIM), kidx, cst)
    return x1, h2, gates, state


def kernel(x_prompt, x_sample, cache_k, cache_v, cache_kidx, state_conv, rel_bias, norm_mix, w_in,
           w_attn_out, w_conv, w_conv_out, w_out, norm_ffn, w_group, b_group, w_router, b_router,
           w_gate, w_up, w_down, norm_final):
    depth = w_in.shape[0]
    assert depth == 1, "the final norm is fused into the last layer's expert kernel"
    l = 0
    w = _prep_weights(w_in[l], w_attn_out[l], w_conv_out[l], w_out[l], w_group[l], b_group[l],
                      w_router[l], b_router[l], w_gate[l], w_up[l], w_down[l])
    outs = []
    states = []
    for x, past in ((x_prompt, None),
                    (x_sample, (cache_k[l], cache_v[l], cache_kidx[l], state_conv[l]))):
        bsz, t, d = x.shape
        x1, h2, gates, state = _layer(x, past, w, norm_mix[l], w_conv[l], norm_ffn[l], rel_bias)
        n = bsz * t
        y = _moe_call(h2, gates, x1, w["wgate"], w["wup"], w["wdown"], norm_final[None, :], _tile(n, 512))
        outs.append(y.reshape(bsz, t, d))
        states.append(state)
    sp, ss = states
    return (outs[0], outs[1], sp[0][None], sp[1][None], sp[2][None], sp[3][None],
            ss[0][None], ss[1][None], ss[2][None], ss[3][None])
```

```python
import functools
import math

import jax
import jax.numpy as jnp
import numpy as np
from jax import lax
from jax.experimental import pallas as pl
from jax.experimental.pallas import tpu as pltpu

F32 = jnp.float32
BF16 = jnp.bfloat16
I32 = jnp.int32

CHUNK = 64
N_HEADS = 8
N_KV_HEADS = 2
GROUP = N_HEADS // N_KV_HEADS
HEAD_DIM = 64
ATTN_DIM = N_HEADS * HEAD_DIM
KV_DIM = N_KV_HEADS * HEAD_DIM
IDX_HEADS = 4
IDX_DIM = 64
TOPK_MAX = 256
CONV_DIM = 512
CONV_W = 3
N_BUCKETS = 32
MAX_DISTANCE = 128
N_GROUPS = 4
EXPERTS_PER_GROUP = 4
N_EXPERTS = N_GROUPS * EXPERTS_PER_GROUP
EPS = 1e-6
ATTN_SCALE = HEAD_DIM ** -0.5
IDX_SCALE = IDX_DIM ** -0.5
IDX_W_SCALE = IDX_HEADS ** -0.5

LANES = 128
SUBLANES = 8
KEY_TILE = 256
ROW_STRIP = 128
MOE_EXPERTS_PER_STEP = 4
VMEM_LIMIT = 56 * 1024 * 1024
INT_MIN = -2 ** 31
IDX_BIG = 2 ** 30
F32_LOWEST = float(np.finfo(np.float32).min)
LOG2E = math.log2(math.e)
BOUND_SLACK = 1.0 + 2.0 ** -5
L_UNDERFLOW = 2.0 ** -80


def _dot(a, b):
    return jnp.dot(a, b, preferred_element_type=F32)


def _dot_nt(a, b):
    return lax.dot_general(a, b, (((1,), (1,)), ((), ())), preferred_element_type=F32)


def _sigmoid(x):
    return 1.0 / (1.0 + jnp.exp(-x))


def _rms(x, g):
    return x * lax.rsqrt(jnp.mean(x * x, axis=-1, keepdims=True) + EPS) * g


def _proj_kernel(x_ref, cinit_ref, g_ref, wa_ref, wb_ref, wc_ref, wg_ref, wconv_ref, wco_ref,
                 q_ref, k_ref, v_ref, kb_ref, vb_ref, qi_ref, kidx_ref, kid_ref, wi_ref,
                 sga_ref, pc_ref, cst_ref, carry_ref, *, tm):
    j = pl.program_id(1)
    hb = _rms(x_ref[0], g_ref[...]).astype(BF16)

    a = _dot(hb, wa_ref[...])
    q_ref[0] = a[:, :ATTN_DIM].astype(BF16)
    k = a[:, ATTN_DIM:ATTN_DIM + KV_DIM]
    v = a[:, ATTN_DIM + KV_DIM:]
    k_ref[0] = k
    v_ref[0] = v
    kb_ref[0] = k.astype(BF16)
    vb_ref[0] = v.astype(BF16)

    b = _dot(hb, wb_ref[...])
    nqi = IDX_HEADS * IDX_DIM
    qi_ref[0] = b[:, :nqi].astype(BF16)
    kid_ref[0] = b[:, nqi:nqi + LANES].astype(BF16)
    kidx_ref[0] = b[:, nqi:nqi + IDX_DIM]
    wi_ref[0] = b[:, nqi + LANES:]

    c = _dot(hb, wc_ref[...])
    cb = c[:, :CONV_DIM]
    u = c[:, CONV_DIM:2 * CONV_DIM] * c[:, 2 * CONV_DIM:]

    @pl.when(j == 0)
    def _():
        carry_ref[6:8, :] = cinit_ref[0]

    c2 = carry_ref[6:7, :]
    c1 = carry_ref[7:8, :]
    row = lax.broadcasted_iota(I32, (tm, CONV_DIM), 0)
    up1 = jnp.where(row == 0, c1, pltpu.roll(u, 1, axis=0))
    up2 = jnp.where(row == 0, c2, jnp.where(row == 1, c1, pltpu.roll(u, 2, axis=0)))
    conv = wconv_ref[0:1, :] * up2 + wconv_ref[1:2, :] * up1 + wconv_ref[2:3, :] * u
    tail = u[tm - 2:tm, :]
    carry_ref[6:8, :] = tail
    cst_ref[0] = tail
    yc = _dot((cb * conv).astype(BF16), wco_ref[...])

    gt = _dot(hb, wg_ref[...])
    d = yc.shape[-1]
    sga_ref[0] = _sigmoid(gt[:, :d])
    pc_ref[0] = _sigmoid(gt[:, d:]) * yc


def _proj_call(x, cinit, g, wa, wb, wc, wg, wconv, wco, tm):
    bsz, s, d = x.shape
    grid = (bsz, s // tm)
    tok = lambda n: pl.BlockSpec((1, tm, n), lambda b, j: (b, j, 0))
    full = lambda a: pl.BlockSpec(a.shape, lambda b, j: (0,) * a.ndim)
    st = pl.BlockSpec((1, CONV_W - 1, CONV_DIM), lambda b, j: (b, 0, 0))
    out_shape = (
        jax.ShapeDtypeStruct((bsz, s, ATTN_DIM), BF16),
        jax.ShapeDtypeStruct((bsz, s, KV_DIM), F32),
        jax.ShapeDtypeStruct((bsz, s, KV_DIM), F32),
        jax.ShapeDtypeStruct((bsz, s, KV_DIM), BF16),
        jax.ShapeDtypeStruct((bsz, s, KV_DIM), BF16),
        jax.ShapeDtypeStruct((bsz, s, IDX_HEADS * IDX_DIM), BF16),
        jax.ShapeDtypeStruct((bsz, s, IDX_DIM), F32),
        jax.ShapeDtypeStruct((bsz, s, LANES), BF16),
        jax.ShapeDtypeStruct((bsz, s, LANES), F32),
        jax.ShapeDtypeStruct((bsz, s, d), F32),
        jax.ShapeDtypeStruct((bsz, s, d), F32),
        jax.ShapeDtypeStruct((bsz, CONV_W - 1, CONV_DIM), F32),
    )
    out_specs = (tok(ATTN_DIM), tok(KV_DIM), tok(KV_DIM), tok(KV_DIM), tok(KV_DIM),
                 tok(IDX_HEADS * IDX_DIM), tok(IDX_DIM), tok(LANES), tok(LANES), tok(d), tok(d), st)
    return pl.pallas_call(
        functools.partial(_proj_kernel, tm=tm),
        grid=grid,
        in_specs=[tok(d), st, full(g), full(wa), full(wb), full(wc), full(wg), full(wconv), full(wco)],
        out_specs=out_specs,
        out_shape=out_shape,
        scratch_shapes=[pltpu.VMEM((8, CONV_DIM), F32)],
        compiler_params=pltpu.CompilerParams(
            dimension_semantics=("arbitrary", "arbitrary"), vmem_limit_bytes=VMEM_LIMIT),
        name="proj",
    )(x, cinit, g, wa, wb, wc, wg, wconv, wco)


def _tree_sum(xs):
    while len(xs) > 1:
        xs = [xs[a] + xs[a + 1] for a in range(0, len(xs) - 1, 2)] + ([xs[-1]] if len(xs) % 2 else [])
    return xs[0]


def _dsa_kernel(qt_ref, qit_ref, wit_ref, kid_ref, kb_ref, vbt_ref, bt_ref, btmax_ref, out_ref,
                keys_ref, qist_ref, qst_ref, eye_ref, jcut_ref, kmax_ref, m_ref, l_ref, acc_ref,
                *, tq, tk, q_off, n_valid_last, topk, idx_bits):
    i = pl.program_id(1)
    nt = (q_off + i * tq) // tk + 1
    neg_inf = F32(-jnp.inf)

    row = lax.broadcasted_iota(I32, (LANES, tq), 0)
    top = row < HEAD_DIM
    zero_b = jnp.zeros((LANES, tq), BF16)
    for h in range(IDX_HEADS):
        blk = qit_ref[0, (h // 2) * LANES:(h // 2 + 1) * LANES, :]
        qist_ref[:, h * tq:(h + 1) * tq] = jnp.where(top if h % 2 == 0 else ~top, blk, zero_b)
    for n in range(N_KV_HEADS):
        for g in range(GROUP):
            blk = qt_ref[0, g * LANES:(g + 1) * LANES, :]
            qst_ref[n, :, g * tq:(g + 1) * tq] = jnp.where(top if n == 0 else ~top, blk, zero_b)
    eye_ref[...] = jnp.where(lax.broadcasted_iota(I32, (tq, tq), 0) == lax.broadcasted_iota(I32, (tq, tq), 1),
                             1.0, 0.0).astype(BF16)
    w_row = jnp.concatenate([wit_ref[0, h:h + 1, :] for h in range(IDX_HEADS)], axis=1)

    krow = lax.broadcasted_iota(I32, (tk, tq), 0)
    qcol = lax.broadcasted_iota(I32, (tk, tq), 1)
    lim_last = jnp.minimum(((qcol >> 6) + 1) << 6, n_valid_last)

    def score_tiles(ts):
        ss = [_dot(kid_ref[0, t], qist_ref[...]) for t in ts]
        for t, s in zip(ts, ss):
            s = jnp.maximum(s, 0.0) * w_row
            sc = (s[:, 0:tq] + s[:, tq:2 * tq]) + (s[:, 2 * tq:3 * tq] + s[:, 3 * tq:4 * tq])
            sc = jnp.where(sc == 0.0, 0.0, sc)
            lim = jnp.where(t == nt - 1, lim_last, tk)
            keys_ref[t] = jnp.where(krow < lim, sc, neg_inf)

    def score_body(j, carry):
        score_tiles([2 * j, 2 * j + 1])
        return carry

    lax.fori_loop(0, nt // 2, score_body, 0)

    @pl.when(nt % 2 == 1)
    def _():
        score_tiles([nt - 1])

    @pl.when(nt % 2 == 1)
    def _():
        keys_ref[nt] = jnp.full((tk, tq), neg_inf, F32)

    def key_to_float(k):
        return lax.bitcast_convert_type(k ^ ((k >> 31) & 0x7FFFFFFF), F32)

    def count(ind_fn, cand):
        cand8 = jnp.broadcast_to(cand, (SUBLANES, tq))
        row8 = lax.broadcasted_iota(I32, (SUBLANES, tq), 0)

        def body(j, acc):
            for t in (2 * j, 2 * j + 1):
                parts = []
                for r in range(tk // SUBLANES):
                    x = keys_ref[t, r * SUBLANES:(r + 1) * SUBLANES, :]
                    parts.append(ind_fn(x, row8 + (t * tk + r * SUBLANES), cand8))
                acc = acc + _tree_sum(parts)
            return acc

        acc = lax.fori_loop(0, (nt + 1) // 2, body, jnp.zeros((SUBLANES, tq), F32))
        return jnp.sum(acc, axis=0, keepdims=True)

    ge = lambda x, idx, c: jnp.where(x >= c, 1.0, 0.0)

    def bit_body(b, thr_k):
        cand = thr_k + (jnp.int32(1) << (31 - b))
        return jnp.where(count(ge, key_to_float(cand)) >= topk, cand, thr_k)

    thr_k = lax.fori_loop(0, 32, bit_body, jnp.full((1, tq), INT_MIN, I32))
    few = thr_k == INT_MIN
    thr = jnp.where(few, neg_inf, key_to_float(thr_k))
    thr8 = jnp.broadcast_to(thr, (SUBLANES, tq))

    jcut_ref[...] = jnp.full(jcut_ref.shape, IDX_BIG, I32)
    n_ge = count(ge, thr)

    @pl.when(jnp.max(n_ge) > topk)
    def _():
        need = topk - count(ge, key_to_float(thr_k + 1))
        tied_lt = lambda x, idx, c: jnp.where(x == thr8, jnp.where(idx < c, 1.0, 0.0), 0.0)

        def jbit_body(b, jj):
            cand = jj + (jnp.int32(1) << (idx_bits - 1 - b))
            return jnp.where(count(tied_lt, cand) <= need, cand, jj)

        jj = lax.fori_loop(0, idx_bits, jbit_body, jnp.zeros((1, tq), I32))
        jcut_ref[...] = jnp.broadcast_to(jj, jcut_ref.shape)

    jcut = jnp.where(few, 0, jcut_ref[0:1, :])

    @pl.when(i == 0)
    def _():
        half0 = lax.broadcasted_iota(I32, (tk, LANES), 1) < HEAD_DIM

        def kbody(t, c):
            k = kb_ref[0, t].astype(F32)
            sq = k * k
            s0 = jnp.max(jnp.sum(jnp.where(half0, sq, 0.0), axis=1, keepdims=True), axis=0, keepdims=True)
            s1 = jnp.max(jnp.sum(jnp.where(half0, 0.0, sq), axis=1, keepdims=True), axis=0, keepdims=True)
            return jnp.maximum(c[0], s0), jnp.maximum(c[1], s1)

        z = jnp.zeros((1, 1), F32)
        c = lax.fori_loop(0, kb_ref.shape[1], kbody, (z, z))
        for n in range(N_KV_HEADS):
            kmax_ref[n:n + 1, :] = jnp.broadcast_to(jnp.sqrt(c[n]), (1, LANES))

    def fold8(x, op):
        parts = [x[r * SUBLANES:(r + 1) * SUBLANES, :] for r in range(x.shape[0] // SUBLANES)]
        while len(parts) > 1:
            parts = [op(parts[a], parts[a + 1]) for a in range(0, len(parts), 2)]
        return parts[0]

    mb = []
    for n in range(N_KV_HEADS):
        qf = qst_ref[n].astype(F32)
        qsq = jnp.sum(fold8(qf * qf, jnp.add), axis=0, keepdims=True)
        mb.append(BOUND_SLACK * jnp.sqrt(qsq) * kmax_ref[n:n + 1, 0:1] + btmax_ref[n, 0:1, :])

    def logits(t, near, n, sel4):
        lg = _dot(kb_ref[0, t], qst_ref[n]) + sel4
        if near is not None:
            lg = lg + bt_ref[near, n]
        return lg

    def selection(t):
        key = keys_ref[t]
        sel = jnp.where(key > thr, 0.0,
                        jnp.where(key == thr, jnp.where(krow + t * tk < jcut, 0.0, neg_inf), neg_inf))
        return jnp.concatenate([sel] * GROUP, axis=1)

    def tiles_fast(specs):
        sel4 = [selection(t) for t, _ in specs]
        units = [(t, near, n) for t, near in specs for n in range(N_KV_HEADS)]
        lgs = [logits(t, near, n, sel4[u // N_KV_HEADS]) for u, (t, near, n) in enumerate(units)]
        ps = [jnp.exp2(lg - mb[n]) for lg, (_, _, n) in zip(lgs, units)]
        for n in range(N_KV_HEADS):
            l_ref[n] = l_ref[n] + _tree_sum([fold8(p, jnp.add) for p, u in zip(ps, units) if u[2] == n])
        pvs = [_dot(vbt_ref[0, t, n * HEAD_DIM:(n + 1) * HEAD_DIM, :], p.astype(BF16))
               for p, (t, _, n) in zip(ps, units)]
        for n in range(N_KV_HEADS):
            acc_ref[n] = acc_ref[n] + _tree_sum([pv for pv, u in zip(pvs, units) if u[2] == n])

    def tiles_exact(specs):
        for t, near in specs:
            tile_exact(t, near)

    def tile_exact(t, near):
        sel4 = selection(t)
        vt = vbt_ref[0, t]
        for n in range(N_KV_HEADS):
            lg = logits(t, near, n, sel4)
            m_old = m_ref[n, 0:1, :]
            m_new = jnp.maximum(m_old, jnp.max(fold8(lg, jnp.maximum), axis=0, keepdims=True))
            m_safe = jnp.maximum(m_new, F32_LOWEST)
            alpha = jnp.exp2(m_old - m_safe)
            p = jnp.exp2(lg - m_safe)
            m_ref[n] = jnp.broadcast_to(m_new, m_ref.shape[1:])
            l_ref[n] = alpha * l_ref[n] + fold8(p, jnp.add)
            pv = _dot(vt[n * HEAD_DIM:(n + 1) * HEAD_DIM, :], p.astype(BF16))
            acc_ref[n] = alpha * acc_ref[n] + pv

    def run_tiles(tiles, unroll):
        l_ref[...] = jnp.zeros(l_ref.shape, F32)
        acc_ref[...] = jnp.zeros(acc_ref.shape, F32)
        n_far = jnp.maximum(nt - 2, 0)

        def far_body(j, carry):
            tiles([(j * unroll + u, None) for u in range(unroll)])
            return carry

        lax.fori_loop(0, n_far // unroll, far_body, 0)
        for u in range(unroll - 1):
            @pl.when(n_far % unroll > u)
            def _():
                tiles([((n_far // unroll) * unroll + u, None)])

        @pl.when(nt >= 2)
        def _():
            tiles([(nt - 2, 0), (nt - 1, 1)])

        @pl.when(nt < 2)
        def _():
            tiles([(nt - 1, 1)])

    run_tiles(tiles_fast, 2)
    l_min = jnp.min(jnp.minimum(jnp.sum(l_ref[0], axis=0, keepdims=True),
                                jnp.sum(l_ref[1], axis=0, keepdims=True)))

    @pl.when(jnp.logical_not(l_min >= L_UNDERFLOW))
    def _():
        m_ref[...] = jnp.full(m_ref.shape, neg_inf, F32)
        run_tiles(tiles_exact, 1)

    o = [acc_ref[n] / jnp.sum(l_ref[n], axis=0, keepdims=True) for n in range(N_KV_HEADS)]
    for g in range(GROUP):
        pair = jnp.concatenate([o[n][:, g * tq:(g + 1) * tq] for n in range(N_KV_HEADS)], axis=0)
        out_ref[0, :, g * LANES:(g + 1) * LANES] = _dot_nt(eye_ref[...], pair.astype(BF16)).astype(BF16)


def _dsa_call(qt, qit, wit, kid, kb, vb, bt, btmax, *, tq, q_off, n_valid_last, topk):
    bsz, _, sq = qt.shape
    nt_max, tk = kid.shape[1], kid.shape[2]
    idx_bits = int(nt_max * tk).bit_length()
    r4 = GROUP * tq
    tok = lambda n: pl.BlockSpec((1, tq, n), lambda b, i: (b, i, 0))
    tokt = lambda n: pl.BlockSpec((1, n, tq), lambda b, i: (b, 0, i))
    keys = pl.BlockSpec((1, nt_max, tk, LANES), lambda b, i: (b, 0, 0, 0))
    kern = functools.partial(_dsa_kernel, tq=tq, tk=tk, q_off=q_off, n_valid_last=n_valid_last,
                             topk=topk, idx_bits=idx_bits)
    return pl.pallas_call(
        kern,
        grid=(bsz, sq // tq),
        in_specs=[tokt(ATTN_DIM), tokt(IDX_HEADS * IDX_DIM), tokt(SUBLANES), keys, keys,
                  pl.BlockSpec((1, nt_max, KV_DIM, tk), lambda b, i: (b, 0, 0, 0)),
                  pl.BlockSpec(bt.shape, lambda b, i: (0, 0, 0, 0)),
                  pl.BlockSpec(btmax.shape, lambda b, i: (0, 0, 0))],
        out_specs=tok(ATTN_DIM),
        out_shape=jax.ShapeDtypeStruct((bsz, sq, ATTN_DIM), BF16),
        scratch_shapes=[
            pltpu.VMEM((nt_max + nt_max % 2, tk, tq), F32),
            pltpu.VMEM((LANES, IDX_HEADS * tq), BF16),
            pltpu.VMEM((N_KV_HEADS, LANES, r4), BF16),
            pltpu.VMEM((tq, tq), BF16),
            pltpu.VMEM((SUBLANES, tq), I32),
            pltpu.VMEM((SUBLANES, LANES), F32),
            pltpu.VMEM((N_KV_HEADS, SUBLANES, r4), F32),
            pltpu.VMEM((N_KV_HEADS, SUBLANES, r4), F32),
            pltpu.VMEM((N_KV_HEADS, HEAD_DIM, r4), F32),
        ],
        compiler_params=pltpu.CompilerParams(
            dimension_semantics=("arbitrary", "arbitrary"), vmem_limit_bytes=VMEM_LIMIT),
        name="dsa",
    )(qt, qit, wit, kid, kb, vb, bt, btmax)


def _t5_bucket(rel):
    nb = N_BUCKETS // 2
    max_exact = nb // 2
    ret = jnp.where(rel > 0, nb, 0)
    n = jnp.abs(rel)
    nf = jnp.maximum(n, 1).astype(F32)
    large = max_exact + (jnp.log(nf / max_exact) / math.log(MAX_DISTANCE / max_exact)
                         * (nb - max_exact)).astype(I32)
    large = jnp.minimum(large, nb - 1)
    return ret + jnp.where(n < max_exact, n, large)


def _bias_tables(rel_bias, tq, tk):
    far = rel_bias[_t5_bucket(jnp.full((1,), -(tk + 1), I32))[0]]
    period = 2 * tk + tq
    off = jnp.arange(period, dtype=I32)
    off = jnp.where(off < 2 * tk, off, off - period)
    diag = ((rel_bias[_t5_bucket(off - tk)] - far) * LOG2E).T
    tab = jnp.tile(diag, (1, tq))[:, :tq * (period - 1)].reshape(N_HEADS, tq, period - 1)[:, :, :2 * tk]
    tab = tab.reshape(N_KV_HEADS, GROUP, tq, 2, tk).transpose(3, 0, 4, 1, 2)
    tab = tab.reshape(2, N_KV_HEADS, tk, GROUP * tq).astype(F32)
    tab_max = jnp.maximum(jnp.max(tab, axis=(0, 2)), 0.0)
    return tab, jnp.broadcast_to(tab_max[:, None, :], (N_KV_HEADS, SUBLANES, GROUP * tq))


def _mix_kernel(attn_ref, sga_ref, pc_ref, x_ref, wao_ref, wout_ref, g_ref, wr_ref, br_ref,
                x1_ref, h2_ref, gates_ref):
    n_sub = 2 if attn_ref.shape[0] % (2 * SUBLANES * 2) == 0 else 1
    ts = attn_ref.shape[0] // n_sub
    subs = [slice(k * ts, (k + 1) * ts) for k in range(n_sub)]
    yas = [_dot(attn_ref[r, :], wao_ref[...]) for r in subs]
    merged = [(sga_ref[r, :] * ya + pc_ref[r, :]).astype(BF16) for r, ya in zip(subs, yas)]
    mixes = [_dot(m, wout_ref[...]) for m in merged]
    x1s = [x_ref[r, :] + mix for r, mix in zip(subs, mixes)]
    hbs = [_rms(x1, g_ref[...]).astype(BF16) for x1 in x1s]
    rls = [_dot(hb, wr_ref[...]) for hb in hbs]
    for r, x1, hb, rl in zip(subs, x1s, hbs, rls):
        x1_ref[r, :] = x1
        h2_ref[r, :] = hb
        gates_ref[r, :] = _route(rl, br_ref)


def _route(rl, br_ref):
    tm = rl.shape[0]
    lane = lax.broadcasted_iota(I32, (tm, LANES), 1)
    neg_inf = F32(-jnp.inf)
    big = I32(LANES)

    def first_argmax(x):
        mx = jnp.max(x, axis=-1, keepdims=True)
        return jnp.min(jnp.where(x == mx, lane, big), axis=-1, keepdims=True)

    def pick(x, idx):
        return jnp.sum(jnp.where(lane == idx, x, 0.0), axis=-1, keepdims=True)

    glog = rl[:, :LANES]
    gvalid = lane < N_GROUPS
    g_sel = first_argmax(jnp.where(gvalid, glog + br_ref[0:1, :], neg_inf))
    gm = jnp.max(jnp.where(gvalid, glog, neg_inf), axis=-1, keepdims=True)
    gexp = jnp.where(gvalid, jnp.exp(glog - gm), 0.0)
    g_prob = pick(gexp, g_sel) / jnp.sum(gexp, axis=-1, keepdims=True)

    elog = rl[:, LANES:]
    in_grp = (lane // EXPERTS_PER_GROUP) == g_sel
    eb = jnp.where(in_grp, elog + br_ref[1:2, :], neg_inf)
    i1 = first_argmax(eb)
    i2 = first_argmax(jnp.where(lane == i1, neg_inf, eb))
    em = jnp.max(jnp.where(in_grp, elog, neg_inf), axis=-1, keepdims=True)
    eexp = jnp.where(in_grp, jnp.exp(elog - em), 0.0)
    esum = jnp.sum(eexp, axis=-1, keepdims=True)
    p1 = pick(eexp, i1) / esum
    p2 = pick(eexp, i2) / esum
    w1 = g_prob * p1 / (p1 + p2)
    w2 = g_prob * p2 / (p1 + p2)
    return jnp.where(lane == i1, w1, 0.0) + jnp.where(lane == i2, w2, 0.0)


def _mix_call(attn, sga, pc, x, wao, wout, g, wr, br, tm):
    n, d = x.shape
    tok = lambda c: pl.BlockSpec((tm, c), lambda i: (i, 0))
    full = lambda a: pl.BlockSpec(a.shape, lambda i: (0,) * a.ndim)
    return pl.pallas_call(
        _mix_kernel,
        grid=(n // tm,),
        in_specs=[tok(ATTN_DIM), tok(d), tok(d), tok(d), full(wao), full(wout), full(g), full(wr), full(br)],
        out_specs=(tok(d), tok(d), tok(LANES)),
        out_shape=(jax.ShapeDtypeStruct((n, d), F32), jax.ShapeDtypeStruct((n, d), BF16),
                   jax.ShapeDtypeStruct((n, LANES), F32)),
        compiler_params=pltpu.CompilerParams(
            dimension_semantics=("arbitrary",), vmem_limit_bytes=VMEM_LIMIT),
        name="mix",
    )(attn, sga, pc, x, wao, wout, g, wr, br)


def _moe_kernel(h_ref, gates_ref, x1_ref, wg_ref, wu_ref, wd_ref, g_ref, out_ref, acc_ref):
    e = pl.program_id(1)

    @pl.when(e == 0)
    def _():
        acc_ref[...] = jnp.zeros(acc_ref.shape, F32)

    hb = h_ref[...]
    n_e = wg_ref.shape[0]
    ups = [(_dot(hb, wg_ref[k]), _dot(hb, wu_ref[k])) for k in range(n_e)]
    acts = [(a * _sigmoid(a) * b).astype(BF16) for a, b in ups]
    ys = [_dot(act, wd_ref[k]) for k, act in enumerate(acts)]
    lane = lax.broadcasted_iota(I32, gates_ref.shape, 1)
    gates = gates_ref[...]
    acc = acc_ref[...]
    for k, y in enumerate(ys):
        ge = jnp.sum(jnp.where(lane == e * n_e + k, gates, 0.0), axis=-1, keepdims=True)
        acc = acc + ge * y
    acc_ref[...] = acc

    @pl.when(e == pl.num_programs(1) - 1)
    def _():
        out_ref[...] = _rms(x1_ref[...] + acc_ref[...], g_ref[...])


def _moe_call(hb, gates, x1, wg, wu, wd, g, tm):
    n, d = x1.shape
    ne, _, de = wg.shape
    tok = lambda c: pl.BlockSpec((tm, c), lambda i, e: (i, 0))
    per = MOE_EXPERTS_PER_STEP
    return pl.pallas_call(
        _moe_kernel,
        grid=(n // tm, ne // per),
        in_specs=[tok(d), tok(LANES), tok(d),
                  pl.BlockSpec((per, d, de), lambda i, e: (e, 0, 0)),
                  pl.BlockSpec((per, d, de), lambda i, e: (e, 0, 0)),
                  pl.BlockSpec((per, de, d), lambda i, e: (e, 0, 0)),
                  pl.BlockSpec(g.shape, lambda i, e: (0, 0))],
        out_specs=tok(d),
        out_shape=jax.ShapeDtypeStruct((n, d), F32),
        scratch_shapes=[pltpu.VMEM((tm, d), F32)],
        compiler_params=pltpu.CompilerParams(
            dimension_semantics=("arbitrary", "arbitrary"), vmem_limit_bytes=VMEM_LIMIT),
        name="moe",
    )(hb, gates, x1, wg, wu, wd, g)


def _head_pair_perm():
    cols = []
    for g in range(GROUP):
        for n in range(N_KV_HEADS):
            h = n * GROUP + g
            cols.extend(range(h * HEAD_DIM, (h + 1) * HEAD_DIM))
    return np.asarray(cols, np.int32)


def _prep_weights(w_in, w_attn_out, w_conv_out, w_out, w_group, b_group, w_router, b_router,
                  w_gate, w_up, w_down):
    d = w_in.shape[0]
    o = np.cumsum([0, ATTN_DIM, KV_DIM, KV_DIM, IDX_HEADS * IDX_DIM, IDX_DIM, IDX_HEADS,
                   CONV_DIM, CONV_DIM, CONV_DIM, d, d])
    perm = _head_pair_perm()
    wq = w_in[:, o[0]:o[1]][:, perm] * (ATTN_SCALE * LOG2E)
    wa = jnp.concatenate([wq, w_in[:, o[1]:o[3]]], axis=1).astype(BF16)
    wki = w_in[:, o[4]:o[5]]
    wwi = jnp.pad(w_in[:, o[5]:o[6]] * IDX_W_SCALE, ((0, 0), (0, LANES - IDX_HEADS)))
    wb = jnp.concatenate([w_in[:, o[3]:o[4]] * IDX_SCALE, wki, wki, wwi], axis=1).astype(BF16)
    wc = w_in[:, o[6]:o[9]].astype(BF16)
    wg = w_in[:, o[9]:o[11]].astype(BF16)
    wr = jnp.concatenate([jnp.pad(w_group, ((0, 0), (0, LANES - N_GROUPS))),
                          jnp.pad(w_router, ((0, 0), (0, LANES - N_EXPERTS)))], axis=1).astype(BF16)
    br = jnp.stack([jnp.pad(b_group, (0, LANES - N_GROUPS)),
                    jnp.pad(b_router, (0, LANES - N_EXPERTS))]).astype(F32)
    return dict(wa=wa, wb=wb, wc=wc, wg=wg, wao=w_attn_out[perm, :].astype(BF16),
                wco=w_conv_out.astype(BF16), wout=w_out.astype(BF16), wr=wr, br=br,
                wgate=w_gate.astype(BF16), wup=w_up.astype(BF16), wdown=w_down.astype(BF16))


def _tile(n, pref):
    t = min(n, pref)
    assert n % t == 0, (n, pref)
    return t


def _layer(x, past, w, norm_mix, w_conv, norm_ffn, rel_bias):
    bsz, t, d = x.shape
    tk = KEY_TILE
    cinit = jnp.zeros((bsz, CONV_W - 1, CONV_DIM), F32) if past is None else past[3].astype(F32)
    (q, k, v, kb, vb, qi, kidx, kid, wi, sga, pc, cst) = _proj_call(
        x, cinit, norm_mix[None, :], w["wa"], w["wb"], w["wc"], w["wg"], w_conv, w["wco"], _tile(t, 512))

    if past is None:
        assert t % tk == 0
        tq, q_off, total = tk, 0, t
        n_valid_last = tk
        kid_all, kb_all, vb_all = kid, kb, vb
    else:
        p_len = past[0].shape[1]
        tq = ROW_STRIP
        assert p_len % tk == 0 and t <= tq
        q_off, total = p_len, p_len + t
        n_valid_last = t
        qpad = ((0, 0), (0, tq - t), (0, 0))
        q, qi, wi = jnp.pad(q, qpad), jnp.pad(qi, qpad), jnp.pad(wi, qpad)
        pad = ((0, 0), (0, tk - t), (0, 0))
        pk = past[0].reshape(bsz, p_len, KV_DIM).astype(BF16)
        pv = past[1].reshape(bsz, p_len, KV_DIM).astype(BF16)
        pki = past[2].astype(BF16)
        kb_all = jnp.concatenate([pk, jnp.pad(kb, pad)], axis=1)
        vb_all = jnp.concatenate([pv, jnp.pad(vb, pad)], axis=1)
        kid_all = jnp.concatenate([jnp.concatenate([pki, pki], axis=-1), jnp.pad(kid, pad)], axis=1)
    nt = kb_all.shape[1] // tk
    tiles = lambda a: a.reshape(bsz, nt, tk, LANES)
    wit = jnp.swapaxes(wi[:, :, :SUBLANES], 1, 2)
    vbt = jnp.swapaxes(tiles(vb_all), 2, 3)
    bt, btmax = _bias_tables(rel_bias, tq, tk)
    attn = _dsa_call(jnp.swapaxes(q, 1, 2), jnp.swapaxes(qi, 1, 2), wit,
                     tiles(kid_all), tiles(kb_all), vbt, bt, btmax, tq=tq, q_off=q_off,
                     n_valid_last=n_valid_last, topk=min(TOPK_MAX, total // 4))
    attn = attn[:, :t]

    n = bsz * t
    tm = _tile(n, 512)
    x1, h2, gates = _mix_call(attn.reshape(n, ATTN_DIM), sga.reshape(n, d), pc.reshape(n, d),
                              x.reshape(n, d), w["wao"], w["wout"], norm_ffn[None, :], w["wr"], w["br"], tm)
    state = (k.reshape(bsz, t, N_KV_HEADS, HEAD_DIM), v.reshape(bsz, t, N_KV_HEADS, HEAD_DIM), kidx, cst)
    return x1, h2, gates, state


def kernel(x_prompt, x_sample, cache_k, cache_v, cache_kidx, state_conv, rel_bias, norm_mix, w_in,
           w_attn_out, w_conv, w_conv_out, w_out, norm_ffn, w_group, b_group, w_router, b_router,
           w_gate, w_up, w_down, norm_final):
    depth = w_in.shape[0]
    assert depth == 1, "the final norm is fused into the last layer's expert kernel"
    l = 0
    w = _prep_weights(w_in[l], w_attn_out[l], w_conv_out[l], w_out[l], w_group[l], b_group[l],
                      w_router[l], b_router[l], w_gate[l], w_up[l], w_down[l])
    outs = []
    states = []
    for x, past in ((x_prompt, None),
                    (x_sample, (cache_k[l], cache_v[l], cache_kidx[l], state_conv[l]))):
        bsz, t, d = x.shape
        x1, h2, gates, state = _layer(x, past, w, norm_mix[l], w_conv[l], norm_ffn[l], rel_bias)
        n = bsz * t
        y = _moe_call(h2, gates, x1, w["wgate"], w["wup"], w["wdown"], norm_final[None, :], _tile(n, 512))
        outs.append(y.reshape(bsz, t, d))
        states.append(state)
    sp, ss = states
    return (outs[0], outs[1], sp[0][None], sp[1][None], sp[2][None], sp[3][None],
            ss[0][None], ss[1][None], ss[2][None], ss[3][None])
```

```python
import functools
import math

import jax
import jax.numpy as jnp
import numpy as np
from jax import lax
from jax.experimental import pallas as pl
from jax.experimental.pallas import tpu as pltpu

F32 = jnp.float32
BF16 = jnp.bfloat16
I32 = jnp.int32

CHUNK = 64
N_HEADS = 8
N_KV_HEADS = 2
GROUP = N_HEADS // N_KV_HEADS
HEAD_DIM = 64
ATTN_DIM = N_HEADS * HEAD_DIM
KV_DIM = N_KV_HEADS * HEAD_DIM
IDX_HEADS = 4
IDX_DIM = 64
TOPK_MAX = 256
CONV_DIM = 512
CONV_W = 3
N_BUCKETS = 32
MAX_DISTANCE = 128
N_GROUPS = 4
EXPERTS_PER_GROUP = 4
N_EXPERTS = N_GROUPS * EXPERTS_PER_GROUP
EPS = 1e-6
ATTN_SCALE = HEAD_DIM ** -0.5
IDX_SCALE = IDX_DIM ** -0.5
IDX_W_SCALE = IDX_HEADS ** -0.5

LANES = 128
SUBLANES = 8
KEY_TILE = 256
ROW_STRIP = 128
MOE_EXPERTS_PER_STEP = 4
VMEM_LIMIT = 56 * 1024 * 1024
INT_MIN = -2 ** 31
IDX_BIG = 2 ** 30
F32_LOWEST = float(np.finfo(np.float32).min)
LOG2E = math.log2(math.e)
BOUND_SLACK = 1.0 + 2.0 ** -5
L_UNDERFLOW = 2.0 ** -80


def _dot(a, b):
    return jnp.dot(a, b, preferred_element_type=F32)


def _dot_nt(a, b):
    return lax.dot_general(a, b, (((1,), (1,)), ((), ())), preferred_element_type=F32)


def _sigmoid(x):
    return 1.0 / (1.0 + jnp.exp(-x))


def _rms(x, g):
    return x * lax.rsqrt(jnp.mean(x * x, axis=-1, keepdims=True) + EPS) * g


def _proj_kernel(x_ref, cinit_ref, g_ref, wa_ref, wb_ref, wc_ref, wg_ref, wconv_ref, wco_ref,
                 q_ref, k_ref, v_ref, kb_ref, vb_ref, qi_ref, kidx_ref, kid_ref, wi_ref,
                 sga_ref, pc_ref, cst_ref, carry_ref, *, tm):
    j = pl.program_id(1)
    hb = _rms(x_ref[0], g_ref[...]).astype(BF16)

    a = _dot(hb, wa_ref[...])
    q_ref[0] = a[:, :ATTN_DIM].astype(BF16)
    k = a[:, ATTN_DIM:ATTN_DIM + KV_DIM]
    v = a[:, ATTN_DIM + KV_DIM:]
    k_ref[0] = k
    v_ref[0] = v
    kb_ref[0] = k.astype(BF16)
    vb_ref[0] = v.astype(BF16)

    b = _dot(hb, wb_ref[...])
    nqi = IDX_HEADS * IDX_DIM
    qi_ref[0] = b[:, :nqi].astype(BF16)
    kid_ref[0] = b[:, nqi:nqi + LANES].astype(BF16)
    kidx_ref[0] = b[:, nqi:nqi + IDX_DIM]
    wi_ref[0] = b[:, nqi + LANES:]

    c = _dot(hb, wc_ref[...])
    cb = c[:, :CONV_DIM]
    u = c[:, CONV_DIM:2 * CONV_DIM] * c[:, 2 * CONV_DIM:]

    @pl.when(j == 0)
    def _():
        carry_ref[6:8, :] = cinit_ref[0]

    c2 = carry_ref[6:7, :]
    c1 = carry_ref[7:8, :]
    row = lax.broadcasted_iota(I32, (tm, CONV_DIM), 0)
    up1 = jnp.where(row == 0, c1, pltpu.roll(u, 1, axis=0))
    up2 = jnp.where(row == 0, c2, jnp.where(row == 1, c1, pltpu.roll(u, 2, axis=0)))
    conv = wconv_ref[0:1, :] * up2 + wconv_ref[1:2, :] * up1 + wconv_ref[2:3, :] * u
    tail = u[tm - 2:tm, :]
    carry_ref[6:8, :] = tail
    cst_ref[0] = tail
    yc = _dot((cb * conv).astype(BF16), wco_ref[...])

    gt = _dot(hb, wg_ref[...])
    d = yc.shape[-1]
    sga_ref[0] = _sigmoid(gt[:, :d])
    pc_ref[0] = _sigmoid(gt[:, d:]) * yc


def _proj_call(x, cinit, g, wa, wb, wc, wg, wconv, wco, tm):
    bsz, s, d = x.shape
    grid = (bsz, s // tm)
    tok = lambda n: pl.BlockSpec((1, tm, n), lambda b, j: (b, j, 0))
    full = lambda a: pl.BlockSpec(a.shape, lambda b, j: (0,) * a.ndim)
    st = pl.BlockSpec((1, CONV_W - 1, CONV_DIM), lambda b, j: (b, 0, 0))
    out_shape = (
        jax.ShapeDtypeStruct((bsz, s, ATTN_DIM), BF16),
        jax.ShapeDtypeStruct((bsz, s, KV_DIM), F32),
        jax.ShapeDtypeStruct((bsz, s, KV_DIM), F32),
        jax.ShapeDtypeStruct((bsz, s, KV_DIM), BF16),
        jax.ShapeDtypeStruct((bsz, s, KV_DIM), BF16),
        jax.ShapeDtypeStruct((bsz, s, IDX_HEADS * IDX_DIM), BF16),
        jax.ShapeDtypeStruct((bsz, s, IDX_DIM), F32),
        jax.ShapeDtypeStruct((bsz, s, LANES), BF16),
        jax.ShapeDtypeStruct((bsz, s, LANES), F32),
        jax.ShapeDtypeStruct((bsz, s, d), F32),
        jax.ShapeDtypeStruct((bsz, s, d), F32),
        jax.ShapeDtypeStruct((bsz, CONV_W - 1, CONV_DIM), F32),
    )
    out_specs = (tok(ATTN_DIM), tok(KV_DIM), tok(KV_DIM), tok(KV_DIM), tok(KV_DIM),
                 tok(IDX_HEADS * IDX_DIM), tok(IDX_DIM), tok(LANES), tok(LANES), tok(d), tok(d), st)
    return pl.pallas_call(
        functools.partial(_proj_kernel, tm=tm),
        grid=grid,
        in_specs=[tok(d), st, full(g), full(wa), full(wb), full(wc), full(wg), full(wconv), full(wco)],
        out_specs=out_specs,
        out_shape=out_shape,
        scratch_shapes=[pltpu.VMEM((8, CONV_DIM), F32)],
        compiler_params=pltpu.CompilerParams(
            dimension_semantics=("arbitrary", "arbitrary"), vmem_limit_bytes=VMEM_LIMIT),
        name="proj",
    )(x, cinit, g, wa, wb, wc, wg, wconv, wco)


def _tree_sum(xs):
    while len(xs) > 1:
        xs = [xs[a] + xs[a + 1] for a in range(0, len(xs) - 1, 2)] + ([xs[-1]] if len(xs) % 2 else [])
    return xs[0]


def _dsa_kernel(qt_ref, qit_ref, wit_ref, kid_ref, kb_ref, vbt_ref, bt_ref, btmax_ref, out_ref,
                keys_ref, qist_ref, qst_ref, eye_ref, jcut_ref, kmax_ref, m_ref, l_ref, acc_ref,
                *, tq, tk, q_off, n_valid_last, topk, idx_bits):
    i = pl.program_id(1)
    nt = (q_off + i * tq) // tk + 1
    neg_inf = F32(-jnp.inf)

    row = lax.broadcasted_iota(I32, (LANES, tq), 0)
    top = row < HEAD_DIM
    zero_b = jnp.zeros((LANES, tq), BF16)
    for h in range(IDX_HEADS):
        blk = qit_ref[0, (h // 2) * LANES:(h // 2 + 1) * LANES, :]
        qist_ref[:, h * tq:(h + 1) * tq] = jnp.where(top if h % 2 == 0 else ~top, blk, zero_b)
    for n in range(N_KV_HEADS):
        for g in range(GROUP):
            blk = qt_ref[0, g * LANES:(g + 1) * LANES, :]
            qst_ref[n, :, g * tq:(g + 1) * tq] = jnp.where(top if n == 0 else ~top, blk, zero_b)
    eye_ref[...] = jnp.where(lax.broadcasted_iota(I32, (tq, tq), 0) == lax.broadcasted_iota(I32, (tq, tq), 1),
                             1.0, 0.0).astype(BF16)
    w_row = jnp.concatenate([wit_ref[0, h:h + 1, :] for h in range(IDX_HEADS)], axis=1)

    krow = lax.broadcasted_iota(I32, (tk, tq), 0)
    qcol = lax.broadcasted_iota(I32, (tk, tq), 1)
    lim_last = jnp.minimum(((qcol >> 6) + 1) << 6, n_valid_last)

    def score_tiles(ts):
        ss = [_dot(kid_ref[0, t], qist_ref[...]) for t in ts]
        for t, s in zip(ts, ss):
            s = jnp.maximum(s, 0.0) * w_row
            sc = (s[:, 0:tq] + s[:, tq:2 * tq]) + (s[:, 2 * tq:3 * tq] + s[:, 3 * tq:4 * tq])
            sc = jnp.where(sc == 0.0, 0.0, sc)
            lim = jnp.where(t == nt - 1, lim_last, tk)
            keys_ref[t] = jnp.where(krow < lim, sc, neg_inf)

    def score_body(j, carry):
        score_tiles([2 * j, 2 * j + 1])
        return carry

    lax.fori_loop(0, nt // 2, score_body, 0)

    @pl.when(nt % 2 == 1)
    def _():
        score_tiles([nt - 1])

    @pl.when(nt % 2 == 1)
    def _():
        keys_ref[nt] = jnp.full((tk, tq), neg_inf, F32)

    def key_to_float(k):
        return lax.bitcast_convert_type(k ^ ((k >> 31) & 0x7FFFFFFF), F32)

    def count(ind_fn, cand):
        cand8 = jnp.broadcast_to(cand, (SUBLANES, tq))
        row8 = lax.broadcasted_iota(I32, (SUBLANES, tq), 0)

        def pair(j, acc):
            for t in (2 * j, 2 * j + 1):
                parts = []
                for r in range(tk // SUBLANES):
                    x = keys_ref[t, r * SUBLANES:(r + 1) * SUBLANES, :]
                    parts.append(ind_fn(x, row8 + (t * tk + r * SUBLANES), cand8))
                acc = acc + _tree_sum(parts)
            return acc

        def body(j, acc):
            return lax.cond(j < (nt + 1) // 2, lambda: pair(j, acc), lambda: acc)

        acc = lax.fori_loop(0, keys_ref.shape[0] // 2, body, jnp.zeros((SUBLANES, tq), F32))
        return jnp.sum(acc, axis=0, keepdims=True)

    ge = lambda x, idx, c: jnp.where(x >= c, 1.0, 0.0)

    def bit_body(b, thr_k):
        cand = thr_k + (jnp.int32(1) << (31 - b))
        return jnp.where(count(ge, key_to_float(cand)) >= topk, cand, thr_k)

    thr_k = lax.fori_loop(0, 32, bit_body, jnp.full((1, tq), INT_MIN, I32))
    few = thr_k == INT_MIN
    thr = jnp.where(few, neg_inf, key_to_float(thr_k))
    thr8 = jnp.broadcast_to(thr, (SUBLANES, tq))

    jcut_ref[...] = jnp.full(jcut_ref.shape, IDX_BIG, I32)
    n_ge = count(ge, thr)

    @pl.when(jnp.max(n_ge) > topk)
    def _():
        need = topk - count(ge, key_to_float(thr_k + 1))
        tied_lt = lambda x, idx, c: jnp.where(x == thr8, jnp.where(idx < c, 1.0, 0.0), 0.0)

        def jbit_body(b, jj):
            cand = jj + (jnp.int32(1) << (idx_bits - 1 - b))
            return jnp.where(count(tied_lt, cand) <= need, cand, jj)

        jj = lax.fori_loop(0, idx_bits, jbit_body, jnp.zeros((1, tq), I32))
        jcut_ref[...] = jnp.broadcast_to(jj, jcut_ref.shape)

    jcut = jnp.where(few, 0, jcut_ref[0:1, :])

    @pl.when(i == 0)
    def _():
        half0 = lax.broadcasted_iota(I32, (tk, LANES), 1) < HEAD_DIM

        def kbody(t, c):
            k = kb_ref[0, t].astype(F32)
            sq = k * k
            s0 = jnp.max(jnp.sum(jnp.where(half0, sq, 0.0), axis=1, keepdims=True), axis=0, keepdims=True)
            s1 = jnp.max(jnp.sum(jnp.where(half0, 0.0, sq), axis=1, keepdims=True), axis=0, keepdims=True)
            return jnp.maximum(c[0], s0), jnp.maximum(c[1], s1)

        z = jnp.zeros((1, 1), F32)
        c = lax.fori_loop(0, kb_ref.shape[1], kbody, (z, z))
        for n in range(N_KV_HEADS):
            kmax_ref[n:n + 1, :] = jnp.broadcast_to(jnp.sqrt(c[n]), (1, LANES))

    def fold8(x, op):
        parts = [x[r * SUBLANES:(r + 1) * SUBLANES, :] for r in range(x.shape[0] // SUBLANES)]
        while len(parts) > 1:
            parts = [op(parts[a], parts[a + 1]) for a in range(0, len(parts), 2)]
        return parts[0]

    mb = []
    for n in range(N_KV_HEADS):
        qf = qst_ref[n].astype(F32)
        qsq = jnp.sum(fold8(qf * qf, jnp.add), axis=0, keepdims=True)
        mb.append(BOUND_SLACK * jnp.sqrt(qsq) * kmax_ref[n:n + 1, 0:1] + btmax_ref[n, 0:1, :])

    def logits(t, near, n, sel4):
        lg = _dot(kb_ref[0, t], qst_ref[n]) + sel4
        if near is not None:
            lg = lg + bt_ref[near, n]
        return lg

    def selection(t):
        key = keys_ref[t]
        sel = jnp.where(key > thr, 0.0,
                        jnp.where(key == thr, jnp.where(krow + t * tk < jcut, 0.0, neg_inf), neg_inf))
        return jnp.concatenate([sel] * GROUP, axis=1)

    def tiles_fast(specs):
        sel4 = [selection(t) for t, _ in specs]
        units = [(t, near, n) for t, near in specs for n in range(N_KV_HEADS)]
        lgs = [logits(t, near, n, sel4[u // N_KV_HEADS]) for u, (t, near, n) in enumerate(units)]
        ps = [jnp.exp2(lg - mb[n]) for lg, (_, _, n) in zip(lgs, units)]
        for n in range(N_KV_HEADS):
            l_ref[n] = l_ref[n] + _tree_sum([fold8(p, jnp.add) for p, u in zip(ps, units) if u[2] == n])
        pvs = [_dot(vbt_ref[0, t, n * HEAD_DIM:(n + 1) * HEAD_DIM, :], p.astype(BF16))
               for p, (t, _, n) in zip(ps, units)]
        for n in range(N_KV_HEADS):
            acc_ref[n] = acc_ref[n] + _tree_sum([pv for pv, u in zip(pvs, units) if u[2] == n])

    def tiles_exact(specs):
        for t, near in specs:
            tile_exact(t, near)

    def tile_exact(t, near):
        sel4 = selection(t)
        vt = vbt_ref[0, t]
        for n in range(N_KV_HEADS):
            lg = logits(t, near, n, sel4)
            m_old = m_ref[n, 0:1, :]
            m_new = jnp.maximum(m_old, jnp.max(fold8(lg, jnp.maximum), axis=0, keepdims=True))
            m_safe = jnp.maximum(m_new, F32_LOWEST)
            alpha = jnp.exp2(m_old - m_safe)
            p = jnp.exp2(lg - m_safe)
            m_ref[n] = jnp.broadcast_to(m_new, m_ref.shape[1:])
            l_ref[n] = alpha * l_ref[n] + fold8(p, jnp.add)
            pv = _dot(vt[n * HEAD_DIM:(n + 1) * HEAD_DIM, :], p.astype(BF16))
            acc_ref[n] = alpha * acc_ref[n] + pv

    def run_tiles(tiles, unroll):
        l_ref[...] = jnp.zeros(l_ref.shape, F32)
        acc_ref[...] = jnp.zeros(acc_ref.shape, F32)
        n_far = jnp.maximum(nt - 2, 0)

        def far_body(j, carry):
            tiles([(j * unroll + u, None) for u in range(unroll)])
            return carry

        lax.fori_loop(0, n_far // unroll, far_body, 0)
        for u in range(unroll - 1):
            @pl.when(n_far % unroll > u)
            def _():
                tiles([((n_far // unroll) * unroll + u, None)])

        @pl.when(nt >= 2)
        def _():
            tiles([(nt - 2, 0), (nt - 1, 1)])

        @pl.when(nt < 2)
        def _():
            tiles([(nt - 1, 1)])

    run_tiles(tiles_fast, 2)
    l_min = jnp.min(jnp.minimum(jnp.sum(l_ref[0], axis=0, keepdims=True),
                                jnp.sum(l_ref[1], axis=0, keepdims=True)))

    @pl.when(jnp.logical_not(l_min >= L_UNDERFLOW))
    def _():
        m_ref[...] = jnp.full(m_ref.shape, neg_inf, F32)
        run_tiles(tiles_exact, 1)

    o = [acc_ref[n] / jnp.sum(l_ref[n], axis=0, keepdims=True) for n in range(N_KV_HEADS)]
    for g in range(GROUP):
        pair = jnp.concatenate([o[n][:, g * tq:(g + 1) * tq] for n in range(N_KV_HEADS)], axis=0)
        out_ref[0, :, g * LANES:(g + 1) * LANES] = _dot_nt(eye_ref[...], pair.astype(BF16)).astype(BF16)


def _dsa_call(qt, qit, wit, kid, kb, vb, bt, btmax, *, tq, q_off, n_valid_last, topk):
    bsz, _, sq = qt.shape
    nt_max, tk = kid.shape[1], kid.shape[2]
    idx_bits = int(nt_max * tk).bit_length()
    r4 = GROUP * tq
    tok = lambda n: pl.BlockSpec((1, tq, n), lambda b, i: (b, i, 0))
    tokt = lambda n: pl.BlockSpec((1, n, tq), lambda b, i: (b, 0, i))
    keys = pl.BlockSpec((1, nt_max, tk, LANES), lambda b, i: (b, 0, 0, 0))
    kern = functools.partial(_dsa_kernel, tq=tq, tk=tk, q_off=q_off, n_valid_last=n_valid_last,
                             topk=topk, idx_bits=idx_bits)
    return pl.pallas_call(
        kern,
        grid=(bsz, sq // tq),
        in_specs=[tokt(ATTN_DIM), tokt(IDX_HEADS * IDX_DIM), tokt(SUBLANES), keys, keys,
                  pl.BlockSpec((1, nt_max, KV_DIM, tk), lambda b, i: (b, 0, 0, 0)),
                  pl.BlockSpec(bt.shape, lambda b, i: (0, 0, 0, 0)),
                  pl.BlockSpec(btmax.shape, lambda b, i: (0, 0, 0))],
        out_specs=tok(ATTN_DIM),
        out_shape=jax.ShapeDtypeStruct((bsz, sq, ATTN_DIM), BF16),
        scratch_shapes=[
            pltpu.VMEM((nt_max + nt_max % 2, tk, tq), F32),
            pltpu.VMEM((LANES, IDX_HEADS * tq), BF16),
            pltpu.VMEM((N_KV_HEADS, LANES, r4), BF16),
            pltpu.VMEM((tq, tq), BF16),
            pltpu.VMEM((SUBLANES, tq), I32),
            pltpu.VMEM((SUBLANES, LANES), F32),
            pltpu.VMEM((N_KV_HEADS, SUBLANES, r4), F32),
            pltpu.VMEM((N_KV_HEADS, SUBLANES, r4), F32),
            pltpu.VMEM((N_KV_HEADS, HEAD_DIM, r4), F32),
        ],
        compiler_params=pltpu.CompilerParams(
            dimension_semantics=("arbitrary", "arbitrary"), vmem_limit_bytes=VMEM_LIMIT),
        name="dsa",
    )(qt, qit, wit, kid, kb, vb, bt, btmax)


def _t5_bucket(rel):
    nb = N_BUCKETS // 2
    max_exact = nb // 2
    ret = jnp.where(rel > 0, nb, 0)
    n = jnp.abs(rel)
    nf = jnp.maximum(n, 1).astype(F32)
    large = max_exact + (jnp.log(nf / max_exact) / math.log(MAX_DISTANCE / max_exact)
                         * (nb - max_exact)).astype(I32)
    large = jnp.minimum(large, nb - 1)
    return ret + jnp.where(n < max_exact, n, large)


def _bias_tables(rel_bias, tq, tk):
    far = rel_bias[_t5_bucket(jnp.full((1,), -(tk + 1), I32))[0]]
    period = 2 * tk + tq
    off = jnp.arange(period, dtype=I32)
    off = jnp.where(off < 2 * tk, off, off - period)
    diag = ((rel_bias[_t5_bucket(off - tk)] - far) * LOG2E).T
    tab = jnp.tile(diag, (1, tq))[:, :tq * (period - 1)].reshape(N_HEADS, tq, period - 1)[:, :, :2 * tk]
    tab = tab.reshape(N_KV_HEADS, GROUP, tq, 2, tk).transpose(3, 0, 4, 1, 2)
    tab = tab.reshape(2, N_KV_HEADS, tk, GROUP * tq).astype(F32)
    tab_max = jnp.maximum(jnp.max(tab, axis=(0, 2)), 0.0)
    return tab, jnp.broadcast_to(tab_max[:, None, :], (N_KV_HEADS, SUBLANES, GROUP * tq))


def _mix_kernel(attn_ref, sga_ref, pc_ref, x_ref, wao_ref, wout_ref, g_ref, wr_ref, br_ref,
                x1_ref, h2_ref, gates_ref):
    n_sub = 2 if attn_ref.shape[0] % (2 * SUBLANES * 2) == 0 else 1
    ts = attn_ref.shape[0] // n_sub
    subs = [slice(k * ts, (k + 1) * ts) for k in range(n_sub)]
    yas = [_dot(attn_ref[r, :], wao_ref[...]) for r in subs]
    merged = [(sga_ref[r, :] * ya + pc_ref[r, :]).astype(BF16) for r, ya in zip(subs, yas)]
    mixes = [_dot(m, wout_ref[...]) for m in merged]
    x1s = [x_ref[r, :] + mix for r, mix in zip(subs, mixes)]
    hbs = [_rms(x1, g_ref[...]).astype(BF16) for x1 in x1s]
    rls = [_dot(hb, wr_ref[...]) for hb in hbs]
    for r, x1, hb, rl in zip(subs, x1s, hbs, rls):
        x1_ref[r, :] = x1
        h2_ref[r, :] = hb
        gates_ref[r, :] = _route(rl, br_ref)


def _route(rl, br_ref):
    tm = rl.shape[0]
    lane = lax.broadcasted_iota(I32, (tm, LANES), 1)
    neg_inf = F32(-jnp.inf)
    big = I32(LANES)

    def first_argmax(x):
        mx = jnp.max(x, axis=-1, keepdims=True)
        return jnp.min(jnp.where(x == mx, lane, big), axis=-1, keepdims=True)

    def pick(x, idx):
        return jnp.sum(jnp.where(lane == idx, x, 0.0), axis=-1, keepdims=True)

    glog = rl[:, :LANES]
    gvalid = lane < N_GROUPS
    g_sel = first_argmax(jnp.where(gvalid, glog + br_ref[0:1, :], neg_inf))
    gm = jnp.max(jnp.where(gvalid, glog, neg_inf), axis=-1, keepdims=True)
    gexp = jnp.where(gvalid, jnp.exp(glog - gm), 0.0)
    g_prob = pick(gexp, g_sel) / jnp.sum(gexp, axis=-1, keepdims=True)

    elog = rl[:, LANES:]
    in_grp = (lane // EXPERTS_PER_GROUP) == g_sel
    eb = jnp.where(in_grp, elog + br_ref[1:2, :], neg_inf)
    i1 = first_argmax(eb)
    i2 = first_argmax(jnp.where(lane == i1, neg_inf, eb))
    em = jnp.max(jnp.where(in_grp, elog, neg_inf), axis=-1, keepdims=True)
    eexp = jnp.where(in_grp, jnp.exp(elog - em), 0.0)
    esum = jnp.sum(eexp, axis=-1, keepdims=True)
    p1 = pick(eexp, i1) / esum
    p2 = pick(eexp, i2) / esum
    w1 = g_prob * p1 / (p1 + p2)
    w2 = g_prob * p2 / (p1 + p2)
    return jnp.where(lane == i1, w1, 0.0) + jnp.where(lane == i2, w2, 0.0)


def _mix_call(attn, sga, pc, x, wao, wout, g, wr, br, tm):
    n, d = x.shape
    tok = lambda c: pl.BlockSpec((tm, c), lambda i: (i, 0))
    full = lambda a: pl.BlockSpec(a.shape, lambda i: (0,) * a.ndim)
    return pl.pallas_call(
        _mix_kernel,
        grid=(n // tm,),
        in_specs=[tok(ATTN_DIM), tok(d), tok(d), tok(d), full(wao), full(wout), full(g), full(wr), full(br)],
        out_specs=(tok(d), tok(d), tok(LANES)),
        out_shape=(jax.ShapeDtypeStruct((n, d), F32), jax.ShapeDtypeStruct((n, d), BF16),
                   jax.ShapeDtypeStruct((n, LANES), F32)),
        compiler_params=pltpu.CompilerParams(
            dimension_semantics=("arbitrary",), vmem_limit_bytes=VMEM_LIMIT),
        name="mix",
    )(attn, sga, pc, x, wao, wout, g, wr, br)


def _moe_kernel(h_ref, gates_ref, x1_ref, wg_ref, wu_ref, wd_ref, g_ref, out_ref, acc_ref):
    e = pl.program_id(1)

    @pl.when(e == 0)
    def _():
        acc_ref[...] = jnp.zeros(acc_ref.shape, F32)

    hb = h_ref[...]
    n_e = wg_ref.shape[0]
    ups = [(_dot(hb, wg_ref[k]), _dot(hb, wu_ref[k])) for k in range(n_e)]
    acts = [(a * _sigmoid(a) * b).astype(BF16) for a, b in ups]
    ys = [_dot(act, wd_ref[k]) for k, act in enumerate(acts)]
    lane = lax.broadcasted_iota(I32, gates_ref.shape, 1)
    gates = gates_ref[...]
    acc = acc_ref[...]
    for k, y in enumerate(ys):
        ge = jnp.sum(jnp.where(lane == e * n_e + k, gates, 0.0), axis=-1, keepdims=True)
        acc = acc + ge * y
    acc_ref[...] = acc

    @pl.when(e == pl.num_programs(1) - 1)
    def _():
        out_ref[...] = _rms(x1_ref[...] + acc_ref[...], g_ref[...])


def _moe_call(hb, gates, x1, wg, wu, wd, g, tm):
    n, d = x1.shape
    ne, _, de = wg.shape
    tok = lambda c: pl.BlockSpec((tm, c), lambda i, e: (i, 0))
    per = MOE_EXPERTS_PER_STEP
    return pl.pallas_call(
        _moe_kernel,
        grid=(n // tm, ne // per),
        in_specs=[tok(d), tok(LANES), tok(d),
                  pl.BlockSpec((per, d, de), lambda i, e: (e, 0, 0)),
                  pl.BlockSpec((per, d, de), lambda i, e: (e, 0, 0)),
                  pl.BlockSpec((per, de, d), lambda i, e: (e, 0, 0)),
                  pl.BlockSpec(g.shape, lambda i, e: (0, 0))],
        out_specs=tok(d),
        out_shape=jax.ShapeDtypeStruct((n, d), F32),
        scratch_shapes=[pltpu.VMEM((tm, d), F32)],
        compiler_params=pltpu.CompilerParams(
            dimension_semantics=("arbitrary", "arbitrary"), vmem_limit_bytes=VMEM_LIMIT),
        name="moe",
    )(hb, gates, x1, wg, wu, wd, g)


def _head_pair_perm():
    cols = []
    for g in range(GROUP):
        for n in range(N_KV_HEADS):
            h = n * GROUP + g
            cols.extend(range(h * HEAD_DIM, (h + 1) * HEAD_DIM))
    return np.asarray(cols, np.int32)


def _prep_weights(w_in, w_attn_out, w_conv_out, w_out, w_group, b_group, w_router, b_router,
                  w_gate, w_up, w_down):
    d = w_in.shape[0]
    o = np.cumsum([0, ATTN_DIM, KV_DIM, KV_DIM, IDX_HEADS * IDX_DIM, IDX_DIM, IDX_HEADS,
                   CONV_DIM, CONV_DIM, CONV_DIM, d, d])
    perm = _head_pair_perm()
    wq = w_in[:, o[0]:o[1]][:, perm] * (ATTN_SCALE * LOG2E)
    wa = jnp.concatenate([wq, w_in[:, o[1]:o[3]]], axis=1).astype(BF16)
    wki = w_in[:, o[4]:o[5]]
    wwi = jnp.pad(w_in[:, o[5]:o[6]] * IDX_W_SCALE, ((0, 0), (0, LANES - IDX_HEADS)))
    wb = jnp.concatenate([w_in[:, o[3]:o[4]] * IDX_SCALE, wki, wki, wwi], axis=1).astype(BF16)
    wc = w_in[:, o[6]:o[9]].astype(BF16)
    wg = w_in[:, o[9]:o[11]].astype(BF16)
    wr = jnp.concatenate([jnp.pad(w_group, ((0, 0), (0, LANES - N_GROUPS))),
                          jnp.pad(w_router, ((0, 0), (0, LANES - N_EXPERTS)))], axis=1).astype(BF16)
    br = jnp.stack([jnp.pad(b_group, (0, LANES - N_GROUPS)),
                    jnp.pad(b_router, (0, LANES - N_EXPERTS))]).astype(F32)
    return dict(wa=wa, wb=wb, wc=wc, wg=wg, wao=w_attn_out[perm, :].astype(BF16),
                wco=w_conv_out.astype(BF16), wout=w_out.astype(BF16), wr=wr, br=br,
                wgate=w_gate.astype(BF16), wup=w_up.astype(BF16), wdown=w_down.astype(BF16))


def _tile(n, pref):
    t = min(n, pref)
    assert n % t == 0, (n, pref)
    return t


def _layer(x, past, w, norm_mix, w_conv, norm_ffn, rel_bias):
    bsz, t, d = x.shape
    tk = KEY_TILE
    cinit = jnp.zeros((bsz, CONV_W - 1, CONV_DIM), F32) if past is None else past[3].astype(F32)
    (q, k, v, kb, vb, qi, kidx, kid, wi, sga, pc, cst) = _proj_call(
        x, cinit, norm_mix[None, :], w["wa"], w["wb"], w["wc"], w["wg"], w_conv, w["wco"], _tile(t, 512))

    if past is None:
        assert t % tk == 0
        tq, q_off, total = tk, 0, t
        n_valid_last = tk
        kid_all, kb_all, vb_all = kid, kb, vb
    else:
        p_len = past[0].shape[1]
        tq = ROW_STRIP
        assert p_len % tk == 0 and t <= tq
        q_off, total = p_len, p_len + t
        n_valid_last = t
        qpad = ((0, 0), (0, tq - t), (0, 0))
        q, qi, wi = jnp.pad(q, qpad), jnp.pad(qi, qpad), jnp.pad(wi, qpad)
        pad = ((0, 0), (0, tk - t), (0, 0))
        pk = past[0].reshape(bsz, p_len, KV_DIM).astype(BF16)
        pv = past[1].reshape(bsz, p_len, KV_DIM).astype(BF16)
        pki = past[2].astype(BF16)
        kb_all = jnp.concatenate([pk, jnp.pad(kb, pad)], axis=1)
        vb_all = jnp.concatenate([pv, jnp.pad(vb, pad)], axis=1)
        kid_all = jnp.concatenate([jnp.concatenate([pki, pki], axis=-1), jnp.pad(kid, pad)], axis=1)
    nt = kb_all.shape[1] // tk
    tiles = lambda a: a.reshape(bsz, nt, tk, LANES)
    wit = jnp.swapaxes(wi[:, :, :SUBLANES], 1, 2)
    vbt = jnp.swapaxes(tiles(vb_all), 2, 3)
    bt, btmax = _bias_tables(rel_bias, tq, tk)
    attn = _dsa_call(jnp.swapaxes(q, 1, 2), jnp.swapaxes(qi, 1, 2), wit,
                     tiles(kid_all), tiles(kb_all), vbt, bt, btmax, tq=tq, q_off=q_off,
                     n_valid_last=n_valid_last, topk=min(TOPK_MAX, total // 4))
    attn = attn[:, :t]

    n = bsz * t
    tm = _tile(n, 512)
    x1, h2, gates = _mix_call(attn.reshape(n, ATTN_DIM), sga.reshape(n, d), pc.reshape(n, d),
                              x.reshape(n, d), w["wao"], w["wout"], norm_ffn[None, :], w["wr"], w["br"], tm)
    state = (k.reshape(bsz, t, N_KV_HEADS, HEAD_DIM), v.reshape(bsz, t, N_KV_HEADS, HEAD_DIM), kidx, cst)
    return x1, h2, gates, state


def kernel(x_prompt, x_sample, cache_k, cache_v, cache_kidx, state_conv, rel_bias, norm_mix, w_in,
           w_attn_out, w_conv, w_conv_out, w_out, norm_ffn, w_group, b_group, w_router, b_router,
           w_gate, w_up, w_down, norm_final):
    depth = w_in.shape[0]
    assert depth == 1, "the final norm is fused into the last layer's expert kernel"
    l = 0
    w = _prep_weights(w_in[l], w_attn_out[l], w_conv_out[l], w_out[l], w_group[l], b_group[l],
                      w_router[l], b_router[l], w_gate[l], w_up[l], w_down[l])
    outs = []
    states = []
    for x, past in ((x_prompt, None),
                    (x_sample, (cache_k[l], cache_v[l], cache_kidx[l], state_conv[l]))):
        bsz, t, d = x.shape
        x1, h2, gates, state = _layer(x, past, w, norm_mix[l], w_conv[l], norm_ffn[l], rel_bias)
        n = bsz * t
        y = _moe_call(h2, gates, x1, w["wgate"], w["wup"], w["wdown"], norm_final[None, :], _tile(n, 512))
        outs.append(y.reshape(bsz, t, d))
        states.append(state)
    sp, ss = states
    return (outs[0], outs[1], sp[0][None], sp[1][None], sp[2][None], sp[3][None],
            ss[0][None], ss[1][None], ss[2][None], ss[3][None])
```

```python
import functools
import math

import jax
import jax.numpy as jnp
import numpy as np
from jax import lax
from jax.experimental import pallas as pl
from jax.experimental.pallas import tpu as pltpu

F32 = jnp.float32
BF16 = jnp.bfloat16
I32 = jnp.int32

CHUNK = 64
N_HEADS = 8
N_KV_HEADS = 2
GROUP = N_HEADS // N_KV_HEADS
HEAD_DIM = 64
ATTN_DIM = N_HEADS * HEAD_DIM
KV_DIM = N_KV_HEADS * HEAD_DIM
IDX_HEADS = 4
IDX_DIM = 64
TOPK_MAX = 256
CONV_DIM = 512
CONV_W = 3
N_BUCKETS = 32
MAX_DISTANCE = 128
N_GROUPS = 4
EXPERTS_PER_GROUP = 4
N_EXPERTS = N_GROUPS * EXPERTS_PER_GROUP
EPS = 1e-6
ATTN_SCALE = HEAD_DIM ** -0.5
IDX_SCALE = IDX_DIM ** -0.5
IDX_W_SCALE = IDX_HEADS ** -0.5

LANES = 128
SUBLANES = 8
KEY_TILE = 256
ROW_STRIP = 128
MOE_EXPERTS_PER_STEP = 4
VMEM_LIMIT = 56 * 1024 * 1024
INT_MIN = -2 ** 31
IDX_BIG = 2 ** 30
F32_LOWEST = float(np.finfo(np.float32).min)
LOG2E = math.log2(math.e)
BOUND_SLACK = 1.0 + 2.0 ** -5
L_UNDERFLOW = 2.0 ** -80
SEARCH_BITS = 24


def _dot(a, b):
    return jnp.dot(a, b, preferred_element_type=F32)


def _dot_nt(a, b):
    return lax.dot_general(a, b, (((1,), (1,)), ((), ())), preferred_element_type=F32)


def _sigmoid(x):
    return 1.0 / (1.0 + jnp.exp(-x))


def _rms(x, g):
    return x * lax.rsqrt(jnp.mean(x * x, axis=-1, keepdims=True) + EPS) * g


def _proj_kernel(x_ref, cinit_ref, g_ref, wa_ref, wb_ref, wc_ref, wg_ref, wconv_ref, wco_ref,
                 q_ref, k_ref, v_ref, kb_ref, vb_ref, qi_ref, kidx_ref, kid_ref, wi_ref,
                 sga_ref, pc_ref, cst_ref, carry_ref, *, tm):
    j = pl.program_id(1)
    hb = _rms(x_ref[0], g_ref[...]).astype(BF16)

    a = _dot(hb, wa_ref[...])
    q_ref[0] = a[:, :ATTN_DIM].astype(BF16)
    k = a[:, ATTN_DIM:ATTN_DIM + KV_DIM]
    v = a[:, ATTN_DIM + KV_DIM:]
    k_ref[0] = k
    v_ref[0] = v
    kb_ref[0] = k.astype(BF16)
    vb_ref[0] = v.astype(BF16)

    b = _dot(hb, wb_ref[...])
    nqi = IDX_HEADS * IDX_DIM
    qi_ref[0] = b[:, :nqi].astype(BF16)
    kid_ref[0] = b[:, nqi:nqi + LANES].astype(BF16)
    kidx_ref[0] = b[:, nqi:nqi + IDX_DIM]
    wi_ref[0] = b[:, nqi + LANES:]

    c = _dot(hb, wc_ref[...])
    cb = c[:, :CONV_DIM]
    u = c[:, CONV_DIM:2 * CONV_DIM] * c[:, 2 * CONV_DIM:]

    @pl.when(j == 0)
    def _():
        carry_ref[6:8, :] = cinit_ref[0]

    c2 = carry_ref[6:7, :]
    c1 = carry_ref[7:8, :]
    row = lax.broadcasted_iota(I32, (tm, CONV_DIM), 0)
    up1 = jnp.where(row == 0, c1, pltpu.roll(u, 1, axis=0))
    up2 = jnp.where(row == 0, c2, jnp.where(row == 1, c1, pltpu.roll(u, 2, axis=0)))
    conv = wconv_ref[0:1, :] * up2 + wconv_ref[1:2, :] * up1 + wconv_ref[2:3, :] * u
    tail = u[tm - 2:tm, :]
    carry_ref[6:8, :] = tail
    cst_ref[0] = tail
    yc = _dot((cb * conv).astype(BF16), wco_ref[...])

    gt = _dot(hb, wg_ref[...])
    d = yc.shape[-1]
    sga_ref[0] = _sigmoid(gt[:, :d])
    pc_ref[0] = _sigmoid(gt[:, d:]) * yc


def _proj_call(x, cinit, g, wa, wb, wc, wg, wconv, wco, tm):
    bsz, s, d = x.shape
    grid = (bsz, s // tm)
    tok = lambda n: pl.BlockSpec((1, tm, n), lambda b, j: (b, j, 0))
    full = lambda a: pl.BlockSpec(a.shape, lambda b, j: (0,) * a.ndim)
    st = pl.BlockSpec((1, CONV_W - 1, CONV_DIM), lambda b, j: (b, 0, 0))
    out_shape = (
        jax.ShapeDtypeStruct((bsz, s, ATTN_DIM), BF16),
        jax.ShapeDtypeStruct((bsz, s, KV_DIM), F32),
        jax.ShapeDtypeStruct((bsz, s, KV_DIM), F32),
        jax.ShapeDtypeStruct((bsz, s, KV_DIM), BF16),
        jax.ShapeDtypeStruct((bsz, s, KV_DIM), BF16),
        jax.ShapeDtypeStruct((bsz, s, IDX_HEADS * IDX_DIM), BF16),
        jax.ShapeDtypeStruct((bsz, s, IDX_DIM), F32),
        jax.ShapeDtypeStruct((bsz, s, LANES), BF16),
        jax.ShapeDtypeStruct((bsz, s, LANES), F32),
        jax.ShapeDtypeStruct((bsz, s, d), F32),
        jax.ShapeDtypeStruct((bsz, s, d), F32),
        jax.ShapeDtypeStruct((bsz, CONV_W - 1, CONV_DIM), F32),
    )
    out_specs = (tok(ATTN_DIM), tok(KV_DIM), tok(KV_DIM), tok(KV_DIM), tok(KV_DIM),
                 tok(IDX_HEADS * IDX_DIM), tok(IDX_DIM), tok(LANES), tok(LANES), tok(d), tok(d), st)
    return pl.pallas_call(
        functools.partial(_proj_kernel, tm=tm),
        grid=grid,
        in_specs=[tok(d), st, full(g), full(wa), full(wb), full(wc), full(wg), full(wconv), full(wco)],
        out_specs=out_specs,
        out_shape=out_shape,
        scratch_shapes=[pltpu.VMEM((8, CONV_DIM), F32)],
        compiler_params=pltpu.CompilerParams(
            dimension_semantics=("arbitrary", "arbitrary"), vmem_limit_bytes=VMEM_LIMIT),
        name="proj",
    )(x, cinit, g, wa, wb, wc, wg, wconv, wco)


def _tree_sum(xs):
    while len(xs) > 1:
        xs = [xs[a] + xs[a + 1] for a in range(0, len(xs) - 1, 2)] + ([xs[-1]] if len(xs) % 2 else [])
    return xs[0]


def _dsa_kernel(qt_ref, qit_ref, wit_ref, kid_ref, kb_ref, vbt_ref, bt_ref, btmax_ref, out_ref,
                keys_ref, qist_ref, qst_ref, eye_ref, jcut_ref, kmax_ref, m_ref, l_ref, acc_ref,
                *, tq, tk, q_off, n_valid_last, topk, idx_bits):
    i = pl.program_id(1)
    nt = (q_off + i * tq) // tk + 1
    neg_inf = F32(-jnp.inf)

    row = lax.broadcasted_iota(I32, (LANES, tq), 0)
    top = row < HEAD_DIM
    zero_b = jnp.zeros((LANES, tq), BF16)
    for h in range(IDX_HEADS):
        blk = qit_ref[0, (h // 2) * LANES:(h // 2 + 1) * LANES, :]
        qist_ref[:, h * tq:(h + 1) * tq] = jnp.where(top if h % 2 == 0 else ~top, blk, zero_b)
    for n in range(N_KV_HEADS):
        for g in range(GROUP):
            blk = qt_ref[0, g * LANES:(g + 1) * LANES, :]
            qst_ref[n, :, g * tq:(g + 1) * tq] = jnp.where(top if n == 0 else ~top, blk, zero_b)
    eye_ref[...] = jnp.where(lax.broadcasted_iota(I32, (tq, tq), 0) == lax.broadcasted_iota(I32, (tq, tq), 1),
                             1.0, 0.0).astype(BF16)
    w_row = jnp.concatenate([wit_ref[0, h:h + 1, :] for h in range(IDX_HEADS)], axis=1)

    krow = lax.broadcasted_iota(I32, (tk, tq), 0)
    qcol = lax.broadcasted_iota(I32, (tk, tq), 1)
    lim_last = jnp.minimum(((qcol >> 6) + 1) << 6, n_valid_last)

    def score_tiles(ts):
        ss = [_dot(kid_ref[0, t], qist_ref[...]) for t in ts]
        for t, s in zip(ts, ss):
            s = jnp.maximum(s, 0.0) * w_row
            sc = (s[:, 0:tq] + s[:, tq:2 * tq]) + (s[:, 2 * tq:3 * tq] + s[:, 3 * tq:4 * tq])
            sc = jnp.where(sc == 0.0, 0.0, sc)
            lim = jnp.where(t == nt - 1, lim_last, tk)
            keys_ref[t] = jnp.where(krow < lim, sc, neg_inf)

    def score_body(j, carry):
        score_tiles([2 * j, 2 * j + 1])
        return carry

    lax.fori_loop(0, nt // 2, score_body, 0)

    @pl.when(nt % 2 == 1)
    def _():
        score_tiles([nt - 1])

    @pl.when(nt % 2 == 1)
    def _():
        keys_ref[nt] = jnp.full((tk, tq), neg_inf, F32)

    def image(k):
        return lax.bitcast_convert_type(k ^ ((k >> 31) & 0x7FFFFFFF), F32)

    def col_min(pick_fn, ref_val):
        ref8 = jnp.broadcast_to(ref_val, (SUBLANES, tq))

        def body(j, acc):
            for t in (2 * j, 2 * j + 1):
                parts = []
                for r in range(tk // SUBLANES):
                    x = keys_ref[t, r * SUBLANES:(r + 1) * SUBLANES, :]
                    parts.append(jnp.where(pick_fn(x, ref8), x, jnp.inf))
                while len(parts) > 1:
                    parts = [jnp.minimum(parts[a], parts[a + 1]) for a in range(0, len(parts), 2)]
                acc = jnp.minimum(acc, parts[0])
            return acc

        acc = lax.fori_loop(0, (nt + 1) // 2, body, jnp.full((SUBLANES, tq), jnp.inf, F32))
        return jnp.min(acc, axis=0, keepdims=True)

    def count(ind_fn, cand):
        cand8 = jnp.broadcast_to(cand, (SUBLANES, tq))
        row8 = lax.broadcasted_iota(I32, (SUBLANES, tq), 0)

        def body(j, acc):
            for t in (2 * j, 2 * j + 1):
                parts = []
                for r in range(tk // SUBLANES):
                    x = keys_ref[t, r * SUBLANES:(r + 1) * SUBLANES, :]
                    parts.append(ind_fn(x, row8 + (t * tk + r * SUBLANES), cand8))
                acc = acc + _tree_sum(parts)
            return acc

        acc = lax.fori_loop(0, (nt + 1) // 2, body, jnp.zeros((SUBLANES, tq), F32))
        return jnp.sum(acc, axis=0, keepdims=True)

    ge = lambda x, idx, c: jnp.where(x >= c, 1.0, 0.0)
    gt = lambda x, idx, c: jnp.where(x > c, 1.0, 0.0)

    def bit_body(b, thr_k):
        cand = thr_k + (jnp.int32(1) << (31 - b))
        return jnp.where(count(ge, image(cand)) >= topk, cand, thr_k)

    thr_k = lax.fori_loop(0, SEARCH_BITS, bit_body, jnp.full((1, tq), INT_MIN, I32))
    few = thr_k == INT_MIN
    thr = jnp.where(few, neg_inf, col_min(lambda x, c: x >= c, image(thr_k)))
    n_gt = count(gt, thr)

    def raise_cond(c):
        return jnp.max(jnp.where(few, 0.0, c[1])) >= topk

    def raise_body(c):
        thr, n_gt = c
        up = jnp.logical_and(n_gt >= topk, jnp.logical_not(few))
        thr = jnp.where(up, col_min(lambda x, c: x > c, thr), thr)
        return thr, count(gt, thr)

    thr, n_gt = lax.while_loop(raise_cond, raise_body, (thr, n_gt))
    thr8 = jnp.broadcast_to(thr, (SUBLANES, tq))

    jcut_ref[...] = jnp.full(jcut_ref.shape, IDX_BIG, I32)
    n_ge = count(ge, thr)

    @pl.when(jnp.max(n_ge) > topk)
    def _():
        need = topk - n_gt
        tied_lt = lambda x, idx, c: jnp.where(x == thr8, jnp.where(idx < c, 1.0, 0.0), 0.0)

        def jbit_body(b, jj):
            cand = jj + (jnp.int32(1) << (idx_bits - 1 - b))
            return jnp.where(count(tied_lt, cand) <= need, cand, jj)

        jj = lax.fori_loop(0, idx_bits, jbit_body, jnp.zeros((1, tq), I32))
        jcut_ref[...] = jnp.broadcast_to(jj, jcut_ref.shape)

    jcut = jnp.where(few, 0, jcut_ref[0:1, :])

    @pl.when(i == 0)
    def _():
        half0 = lax.broadcasted_iota(I32, (tk, LANES), 1) < HEAD_DIM

        def kbody(t, c):
            k = kb_ref[0, t].astype(F32)
            sq = k * k
            s0 = jnp.max(jnp.sum(jnp.where(half0, sq, 0.0), axis=1, keepdims=True), axis=0, keepdims=True)
            s1 = jnp.max(jnp.sum(jnp.where(half0, 0.0, sq), axis=1, keepdims=True), axis=0, keepdims=True)
            return jnp.maximum(c[0], s0), jnp.maximum(c[1], s1)

        z = jnp.zeros((1, 1), F32)
        c = lax.fori_loop(0, kb_ref.shape[1], kbody, (z, z))
        for n in range(N_KV_HEADS):
            kmax_ref[n:n + 1, :] = jnp.broadcast_to(jnp.sqrt(c[n]), (1, LANES))

    def fold8(x, op):
        parts = [x[r * SUBLANES:(r + 1) * SUBLANES, :] for r in range(x.shape[0] // SUBLANES)]
        while len(parts) > 1:
            parts = [op(parts[a], parts[a + 1]) for a in range(0, len(parts), 2)]
        return parts[0]

    mb = []
    for n in range(N_KV_HEADS):
        qf = qst_ref[n].astype(F32)
        qsq = jnp.sum(fold8(qf * qf, jnp.add), axis=0, keepdims=True)
        mb.append(BOUND_SLACK * jnp.sqrt(qsq) * kmax_ref[n:n + 1, 0:1] + btmax_ref[n, 0:1, :])

    def logits(t, near, n, sel4):
        lg = _dot(kb_ref[0, t], qst_ref[n]) + sel4
        if near is not None:
            lg = lg + bt_ref[near, n]
        return lg

    def selection(t):
        key = keys_ref[t]
        sel = jnp.where(key > thr, 0.0,
                        jnp.where(key == thr, jnp.where(krow + t * tk < jcut, 0.0, neg_inf), neg_inf))
        return jnp.concatenate([sel] * GROUP, axis=1)

    def tiles_fast(specs):
        sel4 = [selection(t) for t, _ in specs]
        units = [(t, near, n) for t, near in specs for n in range(N_KV_HEADS)]
        lgs = [logits(t, near, n, sel4[u // N_KV_HEADS]) for u, (t, near, n) in enumerate(units)]
        ps = [jnp.exp2(lg - mb[n]) for lg, (_, _, n) in zip(lgs, units)]
        for n in range(N_KV_HEADS):
            l_ref[n] = l_ref[n] + _tree_sum([fold8(p, jnp.add) for p, u in zip(ps, units) if u[2] == n])
        pvs = [_dot(vbt_ref[0, t, n * HEAD_DIM:(n + 1) * HEAD_DIM, :], p.astype(BF16))
               for p, (t, _, n) in zip(ps, units)]
        for n in range(N_KV_HEADS):
            acc_ref[n] = acc_ref[n] + _tree_sum([pv for pv, u in zip(pvs, units) if u[2] == n])

    def tiles_exact(specs):
        for t, near in specs:
            tile_exact(t, near)

    def tile_exact(t, near):
        sel4 = selection(t)
        vt = vbt_ref[0, t]
        for n in range(N_KV_HEADS):
            lg = logits(t, near, n, sel4)
            m_old = m_ref[n, 0:1, :]
            m_new = jnp.maximum(m_old, jnp.max(fold8(lg, jnp.maximum), axis=0, keepdims=True))
            m_safe = jnp.maximum(m_new, F32_LOWEST)
            alpha = jnp.exp2(m_old - m_safe)
            p = jnp.exp2(lg - m_safe)
            m_ref[n] = jnp.broadcast_to(m_new, m_ref.shape[1:])
            l_ref[n] = alpha * l_ref[n] + fold8(p, jnp.add)
            pv = _dot(vt[n * HEAD_DIM:(n + 1) * HEAD_DIM, :], p.astype(BF16))
            acc_ref[n] = alpha * acc_ref[n] + pv

    def run_tiles(tiles, unroll):
        l_ref[...] = jnp.zeros(l_ref.shape, F32)
        acc_ref[...] = jnp.zeros(acc_ref.shape, F32)
        n_far = jnp.maximum(nt - 2, 0)

        def far_body(j, carry):
            tiles([(j * unroll + u, None) for u in range(unroll)])
            return carry

        lax.fori_loop(0, n_far // unroll, far_body, 0)
        for u in range(unroll - 1):
            @pl.when(n_far % unroll > u)
            def _():
                tiles([((n_far // unroll) * unroll + u, None)])

        @pl.when(nt >= 2)
        def _():
            tiles([(nt - 2, 0), (nt - 1, 1)])

        @pl.when(nt < 2)
        def _():
            tiles([(nt - 1, 1)])

    run_tiles(tiles_fast, 2)
    l_min = jnp.min(jnp.minimum(jnp.sum(l_ref[0], axis=0, keepdims=True),
                                jnp.sum(l_ref[1], axis=0, keepdims=True)))

    @pl.when(jnp.logical_not(l_min >= L_UNDERFLOW))
    def _():
        m_ref[...] = jnp.full(m_ref.shape, neg_inf, F32)
        run_tiles(tiles_exact, 1)

    o = [acc_ref[n] / jnp.sum(l_ref[n], axis=0, keepdims=True) for n in range(N_KV_HEADS)]
    for g in range(GROUP):
        pair = jnp.concatenate([o[n][:, g * tq:(g + 1) * tq] for n in range(N_KV_HEADS)], axis=0)
        out_ref[0, :, g * LANES:(g + 1) * LANES] = _dot_nt(eye_ref[...], pair.astype(BF16)).astype(BF16)


def _dsa_call(qt, qit, wit, kid, kb, vb, bt, btmax, *, tq, q_off, n_valid_last, topk):
    bsz, _, sq = qt.shape
    nt_max, tk = kid.shape[1], kid.shape[2]
    idx_bits = int(nt_max * tk).bit_length()
    r4 = GROUP * tq
    tok = lambda n: pl.BlockSpec((1, tq, n), lambda b, i: (b, i, 0))
    tokt = lambda n: pl.BlockSpec((1, n, tq), lambda b, i: (b, 0, i))
    keys = pl.BlockSpec((1, nt_max, tk, LANES), lambda b, i: (b, 0, 0, 0))
    kern = functools.partial(_dsa_kernel, tq=tq, tk=tk, q_off=q_off, n_valid_last=n_valid_last,
                             topk=topk, idx_bits=idx_bits)
    return pl.pallas_call(
        kern,
        grid=(bsz, sq // tq),
        in_specs=[tokt(ATTN_DIM), tokt(IDX_HEADS * IDX_DIM), tokt(SUBLANES), keys, keys,
                  pl.BlockSpec((1, nt_max, KV_DIM, tk), lambda b, i: (b, 0, 0, 0)),
                  pl.BlockSpec(bt.shape, lambda b, i: (0, 0, 0, 0)),
                  pl.BlockSpec(btmax.shape, lambda b, i: (0, 0, 0))],
        out_specs=tok(ATTN_DIM),
        out_shape=jax.ShapeDtypeStruct((bsz, sq, ATTN_DIM), BF16),
        scratch_shapes=[
            pltpu.VMEM((nt_max + nt_max % 2, tk, tq), F32),
            pltpu.VMEM((LANES, IDX_HEADS * tq), BF16),
            pltpu.VMEM((N_KV_HEADS, LANES, r4), BF16),
            pltpu.VMEM((tq, tq), BF16),
            pltpu.VMEM((SUBLANES, tq), I32),
            pltpu.VMEM((SUBLANES, LANES), F32),
            pltpu.VMEM((N_KV_HEADS, SUBLANES, r4), F32),
            pltpu.VMEM((N_KV_HEADS, SUBLANES, r4), F32),
            pltpu.VMEM((N_KV_HEADS, HEAD_DIM, r4), F32),
        ],
        compiler_params=pltpu.CompilerParams(
            dimension_semantics=("arbitrary", "arbitrary"), vmem_limit_bytes=VMEM_LIMIT),
        name="dsa",
    )(qt, qit, wit, kid, kb, vb, bt, btmax)


def _t5_bucket(rel):
    nb = N_BUCKETS // 2
    max_exact = nb // 2
    ret = jnp.where(rel > 0, nb, 0)
    n = jnp.abs(rel)
    nf = jnp.maximum(n, 1).astype(F32)
    large = max_exact + (jnp.log(nf / max_exact) / math.log(MAX_DISTANCE / max_exact)
                         * (nb - max_exact)).astype(I32)
    large = jnp.minimum(large, nb - 1)
    return ret + jnp.where(n < max_exact, n, large)


def _bias_tables(rel_bias, tq, tk):
    far = rel_bias[_t5_bucket(jnp.full((1,), -(tk + 1), I32))[0]]
    period = 2 * tk + tq
    off = jnp.arange(period, dtype=I32)
    off = jnp.where(off < 2 * tk, off, off - period)
    diag = ((rel_bias[_t5_bucket(off - tk)] - far) * LOG2E).T
    tab = jnp.tile(diag, (1, tq))[:, :tq * (period - 1)].reshape(N_HEADS, tq, period - 1)[:, :, :2 * tk]
    tab = tab.reshape(N_KV_HEADS, GROUP, tq, 2, tk).transpose(3, 0, 4, 1, 2)
    tab = tab.reshape(2, N_KV_HEADS, tk, GROUP * tq).astype(F32)
    tab_max = jnp.maximum(jnp.max(tab, axis=(0, 2)), 0.0)
    return tab, jnp.broadcast_to(tab_max[:, None, :], (N_KV_HEADS, SUBLANES, GROUP * tq))


def _mix_kernel(attn_ref, sga_ref, pc_ref, x_ref, wao_ref, wout_ref, g_ref, wr_ref, br_ref,
                x1_ref, h2_ref, gates_ref):
    n_sub = 2 if attn_ref.shape[0] % (2 * SUBLANES * 2) == 0 else 1
    ts = attn_ref.shape[0] // n_sub
    subs = [slice(k * ts, (k + 1) * ts) for k in range(n_sub)]
    yas = [_dot(attn_ref[r, :], wao_ref[...]) for r in subs]
    merged = [(sga_ref[r, :] * ya + pc_ref[r, :]).astype(BF16) for r, ya in zip(subs, yas)]
    mixes = [_dot(m, wout_ref[...]) for m in merged]
    x1s = [x_ref[r, :] + mix for r, mix in zip(subs, mixes)]
    hbs = [_rms(x1, g_ref[...]).astype(BF16) for x1 in x1s]
    rls = [_dot(hb, wr_ref[...]) for hb in hbs]
    for r, x1, hb, rl in zip(subs, x1s, hbs, rls):
        x1_ref[r, :] = x1
        h2_ref[r, :] = hb
        gates_ref[r, :] = _route(rl, br_ref)


def _route(rl, br_ref):
    tm = rl.shape[0]
    lane = lax.broadcasted_iota(I32, (tm, LANES), 1)
    neg_inf = F32(-jnp.inf)
    big = I32(LANES)

    def first_argmax(x):
        mx = jnp.max(x, axis=-1, keepdims=True)
        return jnp.min(jnp.where(x == mx, lane, big), axis=-1, keepdims=True)

    def pick(x, idx):
        return jnp.sum(jnp.where(lane == idx, x, 0.0), axis=-1, keepdims=True)

    glog = rl[:, :LANES]
    gvalid = lane < N_GROUPS
    g_sel = first_argmax(jnp.where(gvalid, glog + br_ref[0:1, :], neg_inf))
    gm = jnp.max(jnp.where(gvalid, glog, neg_inf), axis=-1, keepdims=True)
    gexp = jnp.where(gvalid, jnp.exp(glog - gm), 0.0)
    g_prob = pick(gexp, g_sel) / jnp.sum(gexp, axis=-1, keepdims=True)

    elog = rl[:, LANES:]
    in_grp = (lane // EXPERTS_PER_GROUP) == g_sel
    eb = jnp.where(in_grp, elog + br_ref[1:2, :], neg_inf)
    i1 = first_argmax(eb)
    i2 = first_argmax(jnp.where(lane == i1, neg_inf, eb))
    em = jnp.max(jnp.where(in_grp, elog, neg_inf), axis=-1, keepdims=True)
    eexp = jnp.where(in_grp, jnp.exp(elog - em), 0.0)
    esum = jnp.sum(eexp, axis=-1, keepdims=True)
    p1 = pick(eexp, i1) / esum
    p2 = pick(eexp, i2) / esum
    w1 = g_prob * p1 / (p1 + p2)
    w2 = g_prob * p2 / (p1 + p2)
    return jnp.where(lane == i1, w1, 0.0) + jnp.where(lane == i2, w2, 0.0)


def _mix_call(attn, sga, pc, x, wao, wout, g, wr, br, tm):
    n, d = x.shape
    tok = lambda c: pl.BlockSpec((tm, c), lambda i: (i, 0))
    full = lambda a: pl.BlockSpec(a.shape, lambda i: (0,) * a.ndim)
    return pl.pallas_call(
        _mix_kernel,
        grid=(n // tm,),
        in_specs=[tok(ATTN_DIM), tok(d), tok(d), tok(d), full(wao), full(wout), full(g), full(wr), full(br)],
        out_specs=(tok(d), tok(d), tok(LANES)),
        out_shape=(jax.ShapeDtypeStruct((n, d), F32), jax.ShapeDtypeStruct((n, d), BF16),
                   jax.ShapeDtypeStruct((n, LANES), F32)),
        compiler_params=pltpu.CompilerParams(
            dimension_semantics=("arbitrary",), vmem_limit_bytes=VMEM_LIMIT),
        name="mix",
    )(attn, sga, pc, x, wao, wout, g, wr, br)


def _moe_kernel(h_ref, gates_ref, x1_ref, wg_ref, wu_ref, wd_ref, g_ref, out_ref, acc_ref):
    e = pl.program_id(1)

    @pl.when(e == 0)
    def _():
        acc_ref[...] = jnp.zeros(acc_ref.shape, F32)

    hb = h_ref[...]
    n_e = wg_ref.shape[0]
    ups = [(_dot(hb, wg_ref[k]), _dot(hb, wu_ref[k])) for k in range(n_e)]
    acts = [(a * _sigmoid(a) * b).astype(BF16) for a, b in ups]
    ys = [_dot(act, wd_ref[k]) for k, act in enumerate(acts)]
    lane = lax.broadcasted_iota(I32, gates_ref.shape, 1)
    gates = gates_ref[...]
    acc = acc_ref[...]
    for k, y in enumerate(ys):
        ge = jnp.sum(jnp.where(lane == e * n_e + k, gates, 0.0), axis=-1, keepdims=True)
        acc = acc + ge * y
    acc_ref[...] = acc

    @pl.when(e == pl.num_programs(1) - 1)
    def _():
        out_ref[...] = _rms(x1_ref[...] + acc_ref[...], g_ref[...])


def _moe_call(hb, gates, x1, wg, wu, wd, g, tm):
    n, d = x1.shape
    ne, _, de = wg.shape
    tok = lambda c: pl.BlockSpec((tm, c), lambda i, e: (i, 0))
    per = MOE_EXPERTS_PER_STEP
    return pl.pallas_call(
        _moe_kernel,
        grid=(n // tm, ne // per),
        in_specs=[tok(d), tok(LANES), tok(d),
                  pl.BlockSpec((per, d, de), lambda i, e: (e, 0, 0)),
                  pl.BlockSpec((per, d, de), lambda i, e: (e, 0, 0)),
                  pl.BlockSpec((per, de, d), lambda i, e: (e, 0, 0)),
                  pl.BlockSpec(g.shape, lambda i, e: (0, 0))],
        out_specs=tok(d),
        out_shape=jax.ShapeDtypeStruct((n, d), F32),
        scratch_shapes=[pltpu.VMEM((tm, d), F32)],
        compiler_params=pltpu.CompilerParams(
            dimension_semantics=("arbitrary", "arbitrary"), vmem_limit_bytes=VMEM_LIMIT),
        name="moe",
    )(hb, gates, x1, wg, wu, wd, g)


def _head_pair_perm():
    cols = []
    for g in range(GROUP):
        for n in range(N_KV_HEADS):
            h = n * GROUP + g
            cols.extend(range(h * HEAD_DIM, (h + 1) * HEAD_DIM))
    return np.asarray(cols, np.int32)


def _prep_weights(w_in, w_attn_out, w_conv_out, w_out, w_group, b_group, w_router, b_router,
                  w_gate, w_up, w_down):
    d = w_in.shape[0]
    o = np.cumsum([0, ATTN_DIM, KV_DIM, KV_DIM, IDX_HEADS * IDX_DIM, IDX_DIM, IDX_HEADS,
                   CONV_DIM, CONV_DIM, CONV_DIM, d, d])
    perm = _head_pair_perm()
    wq = w_in[:, o[0]:o[1]][:, perm] * (ATTN_SCALE * LOG2E)
    wa = jnp.concatenate([wq, w_in[:, o[1]:o[3]]], axis=1).astype(BF16)
    wki = w_in[:, o[4]:o[5]]
    wwi = jnp.pad(w_in[:, o[5]:o[6]] * IDX_W_SCALE, ((0, 0), (0, LANES - IDX_HEADS)))
    wb = jnp.concatenate([w_in[:, o[3]:o[4]] * IDX_SCALE, wki, wki, wwi], axis=1).astype(BF16)
    wc = w_in[:, o[6]:o[9]].astype(BF16)
    wg = w_in[:, o[9]:o[11]].astype(BF16)
    wr = jnp.concatenate([jnp.pad(w_group, ((0, 0), (0, LANES - N_GROUPS))),
                          jnp.pad(w_router, ((0, 0), (0, LANES - N_EXPERTS)))], axis=1).astype(BF16)
    br = jnp.stack([jnp.pad(b_group, (0, LANES - N_GROUPS)),
                    jnp.pad(b_router, (0, LANES - N_EXPERTS))]).astype(F32)
    return dict(wa=wa, wb=wb, wc=wc, wg=wg, wao=w_attn_out[perm, :].astype(BF16),
                wco=w_conv_out.astype(BF16), wout=w_out.astype(BF16), wr=wr, br=br,
                wgate=w_gate.astype(BF16), wup=w_up.astype(BF16), wdown=w_down.astype(BF16))


def _tile(n, pref):
    t = min(n, pref)
    assert n % t == 0, (n, pref)
    return t


def _layer(x, past, w, norm_mix, w_conv, norm_ffn, rel_bias):
    bsz, t, d = x.shape
    tk = KEY_TILE
    cinit = jnp.zeros((bsz, CONV_W - 1, CONV_DIM), F32) if past is None else past[3].astype(F32)
    (q, k, v, kb, vb, qi, kidx, kid, wi, sga, pc, cst) = _proj_call(
        x, cinit, norm_mix[None, :], w["wa"], w["wb"], w["wc"], w["wg"], w_conv, w["wco"], _tile(t, 512))

    if past is None:
        assert t % tk == 0
        tq, q_off, total = tk, 0, t
        n_valid_last = tk
        kid_all, kb_all, vb_all = kid, kb, vb
    else:
        p_len = past[0].shape[1]
        tq = ROW_STRIP
        assert p_len % tk == 0 and t <= tq
        q_off, total = p_len, p_len + t
        n_valid_last = t
        qpad = ((0, 0), (0, tq - t), (0, 0))
        q, qi, wi = jnp.pad(q, qpad), jnp.pad(qi, qpad), jnp.pad(wi, qpad)
        pad = ((0, 0), (0, tk - t), (0, 0))
        pk = past[0].reshape(bsz, p_len, KV_DIM).astype(BF16)
        pv = past[1].reshape(bsz, p_len, KV_DIM).astype(BF16)
        pki = past[2].astype(BF16)
        kb_all = jnp.concatenate([pk, jnp.pad(kb, pad)], axis=1)
        vb_all = jnp.concatenate([pv, jnp.pad(vb, pad)], axis=1)
        kid_all = jnp.concatenate([jnp.concatenate([pki, pki], axis=-1), jnp.pad(kid, pad)], axis=1)
    nt = kb_all.shape[1] // tk
    tiles = lambda a: a.reshape(bsz, nt, tk, LANES)
    wit = jnp.swapaxes(wi[:, :, :SUBLANES], 1, 2)
    vbt = jnp.swapaxes(tiles(vb_all), 2, 3)
    bt, btmax = _bias_tables(rel_bias, tq, tk)
    attn = _dsa_call(jnp.swapaxes(q, 1, 2), jnp.swapaxes(qi, 1, 2), wit,
                     tiles(kid_all), tiles(kb_all), vbt, bt, btmax, tq=tq, q_off=q_off,
                     n_valid_last=n_valid_last, topk=min(TOPK_MAX, total // 4))
    attn = attn[:, :t]

    n = bsz * t
    tm = _tile(n, 512)
    x1, h2, gates = _mix_call(attn.reshape(n, ATTN_DIM), sga.reshape(n, d), pc.reshape(n, d),
                              x.reshape(n, d), w["wao"], w["wout"], norm_ffn[None, :], w["wr"], w["br"], tm)
    state = (k.reshape(bsz, t, N_KV_HEADS, HEAD_DIM), v.reshape(bsz, t, N_KV_HEADS, HEAD_DIM), kidx, cst)
    return x1, h2, gates, state


def kernel(x_prompt, x_sample, cache_k, cache_v, cache_kidx, state_conv, rel_bias, norm_mix, w_in,
           w_attn_out, w_conv, w_conv_out, w_out, norm_ffn, w_group, b_group, w_router, b_router,
           w_gate, w_up, w_down, norm_final):
    depth = w_in.shape[0]
    assert depth == 1, "the final norm is fused into the last layer's expert kernel"
    l = 0
    w = _prep_weights(w_in[l], w_attn_out[l], w_conv_out[l], w_out[l], w_group[l], b_group[l],
                      w_router[l], b_router[l], w_gate[l], w_up[l], w_down[l])
    outs = []
    states = []
    for x, past in ((x_prompt, None),
                    (x_sample, (cache_k[l], cache_v[l], cache_kidx[l], state_conv[l]))):
        bsz, t, d = x.shape
        x1, h2, gates, state = _layer(x, past, w, norm_mix[l], w_conv[l], norm_ffn[l], rel_bias)
        n = bsz * t
        y = _moe_call(h2, gates, x1, w["wgate"], w["wup"], w["wdown"], norm_final[None, :], _tile(n, 512))
        outs.append(y.reshape(bsz, t, d))
        states.append(state)
    sp, ss = states
    return (outs[0], outs[1], sp[0][None], sp[1][None], sp[2][None], sp[3][None],
            ss[0][None], ss[1][None], ss[2][None], ss[3][None])
```
